```python
import jax, jax.numpy as jnp
from jax import lax
import numpy as np


D_MODEL = 1024
BATCH = 1
SEQ = 16384
DEPTH = 4

N_MEM = 256
RET_HEADS = 4
RET_HEAD_DIM = 128
RET_WIDTH = RET_HEADS * RET_HEAD_DIM
RET_CHUNK = 128
MLA_HEADS = 4
MLA_NOPE = 128
MLA_ROPE = 64
MLA_V = 64
MLA_Q_RANK = 256
MLA_KV_RANK = 128
MLA_WIDTH = MLA_HEADS * MLA_V
MEM_HEADS = 4
MEM_HEAD_DIM = 64
MEM_WIDTH = MEM_HEADS * MEM_HEAD_DIM
MIX_WIDTH = RET_WIDTH + MLA_WIDTH + MEM_WIDTH
IN_SPLITS = (RET_WIDTH, RET_WIDTH, RET_WIDTH, RET_WIDTH, MLA_Q_RANK, MLA_KV_RANK, MLA_ROPE, MEM_WIDTH)
IN_WIDTH = sum(IN_SPLITS)
D_FF = -(-8 * D_MODEL // (3 * 256)) * 256
ATTN_BLOCK = 128
ROPE_BASE = 10000.0
EPS = 1e-6

kernel_name = 'hymba_style_retention_mla_memory_hybrid'


def rmsnorm(x, g):
    xf = x.astype(jnp.float32)
    y = xf * lax.rsqrt(jnp.mean(xf * xf, axis=-1, keepdims=True) + EPS)
    return (y * g.astype(jnp.float32)).astype(x.dtype)


def head_rms(x):
    xf = x.astype(jnp.float32)
    return xf * lax.rsqrt(jnp.mean(xf * xf, axis=-1, keepdims=True) + EPS)


def rope(x, pos):
    half = x.shape[-1] // 2
    inv = ROPE_BASE ** (-jnp.arange(half, dtype=jnp.float32) / half)
    ang = pos.astype(jnp.float32)[..., None] * inv
    cos = jnp.cos(ang)[:, :, None, :]
    sin = jnp.sin(ang)[:, :, None, :]
    x1 = x[..., :half].astype(jnp.float32)
    x2 = x[..., half:].astype(jnp.float32)
    return jnp.concatenate([x1 * cos - x2 * sin, x2 * cos + x1 * sin], axis=-1).astype(x.dtype)


def split_cols(p):
    idx = np.cumsum(IN_SPLITS)[:-1].tolist()
    return jnp.split(p, idx, axis=-1)


def retention(q, k, v):
    B, S, H, dk = q.shape
    dv = v.shape[-1]
    C = RET_CHUNK
    n = S // C
    f32 = jnp.float32
    lg = jnp.log(1.0 - 2.0 ** (-5.0 - jnp.arange(H, dtype=f32)))

    def chunks(t):
        return t.astype(f32).reshape(B, n, C, H, t.shape[-1]).transpose(0, 3, 1, 2, 4)

    qc = chunks(q) * (dk ** -0.5)
    kc = chunks(k)
    vc = chunks(v)
    idx = jnp.arange(C, dtype=f32)
    diff = idx[:, None] - idx[None, :]
    decay = jnp.where(diff >= 0, jnp.exp(lg[:, None, None] * jnp.maximum(diff, 0.0)), 0.0)
    scores = jnp.einsum('bhnqd,bhnkd->bhnqk', qc, kc) * decay[None, :, None]
    inner = jnp.einsum('bhnqk,bhnkv->bhnqv', scores, vc)
    zeta = jnp.exp(lg[:, None] * (C - 1.0 - idx)[None])
    upd = jnp.einsum('bhnkd,bhnkv->nbhdv', kc * zeta[None, :, None, :, None], vc)
    chunk_decay = jnp.exp(lg * C)[None, :, None, None]

    def step(state, u):
        return chunk_decay * state + u, state

    _, prev = lax.scan(step, jnp.zeros((B, H, dk, dv), f32), upd)
    xi = jnp.exp(lg[:, None] * (idx + 1.0)[None])
    cross = jnp.einsum('bhnqd,nbhdv->bhnqv', qc, prev) * xi[None, :, None, :, None]
    return (inner + cross).transpose(0, 2, 3, 1, 4).reshape(B, S, H, dv)


def retention_group(rq, rk, rv, rg, pos):
    B, S, _ = rq.shape
    shp = (B, S, RET_HEADS, RET_HEAD_DIM)
    q = rope(rq.reshape(shp), pos)
    k = rope(rk.reshape(shp), pos)
    o = head_rms(retention(q, k, rv.reshape(shp))).reshape(B, S, RET_WIDTH)
    return (jax.nn.silu(rg.astype(jnp.float32)) * o).astype(rq.dtype)


def mla_group(cq_pre, ckv_pre, kr_pre, q_norm_g, w_uq, kv_norm_g, w_ukv, pos):
    B, S, _ = cq_pre.shape
    H = MLA_HEADS
    q = (rmsnorm(cq_pre, q_norm_g) @ w_uq).reshape(B, S, H, MLA_NOPE + MLA_ROPE)
    q_nope = q[..., :MLA_NOPE]
    q_rope = rope(q[..., MLA_NOPE:], pos)
    kv = (rmsnorm(ckv_pre, kv_norm_g) @ w_ukv).reshape(B, S, H, MLA_NOPE + MLA_V)
    k_nope = kv[..., :MLA_NOPE]
    v = kv[..., MLA_NOPE:]
    k_rope = rope(kr_pre[:, :, None, :], pos)[:, :, 0, :]
    scale = (MLA_NOPE + MLA_ROPE) ** -0.5
    nb = S // ATTN_BLOCK
    qn_b = (q_nope * scale).reshape(B, nb, ATTN_BLOCK, H, MLA_NOPE).transpose(1, 0, 2, 3, 4)
    qr_b = (q_rope * scale).reshape(B, nb, ATTN_BLOCK, H, MLA_ROPE).transpose(1, 0, 2, 3, 4)
    kpos = jnp.arange(S, dtype=jnp.int32)

    def block(args):
        qn, qr, i = args
        s = jnp.einsum('bqhd,bkhd->bhqk', qn, k_nope) + jnp.einsum('bqhr,bkr->bhqk', qr, k_rope)
        qpos = i * ATTN_BLOCK + jnp.arange(ATTN_BLOCK, dtype=jnp.int32)
        s = jnp.where(kpos[None, :] <= qpos[:, None], s.astype(jnp.float32), jnp.finfo(jnp.float32).min)
        p = jax.nn.softmax(s, axis=-1).astype(v.dtype)
        return jnp.einsum('bhqk,bkhv->bqhv', p, v)

    o = lax.map(block, (qn_b, qr_b, jnp.arange(nb, dtype=jnp.int32)))
    return o.transpose(1, 0, 2, 3, 4).reshape(B, S, MLA_WIDTH)


def memory_group(mq, mem, mem_g, w_mem_kv):
    B, S, _ = mq.shape
    M = mem.shape[1]
    kv = rmsnorm(mem, mem_g) @ w_mem_kv
    k = kv[..., :MEM_WIDTH].reshape(B, M, MEM_HEADS, MEM_HEAD_DIM)
    v = kv[..., MEM_WIDTH:].reshape(B, M, MEM_HEADS, MEM_HEAD_DIM)
    q = mq.reshape(B, S, MEM_HEADS, MEM_HEAD_DIM) * (MEM_HEAD_DIM ** -0.5)
    s = jnp.einsum('bshd,bmhd->bhsm', q, k).astype(jnp.float32)
    p = jax.nn.softmax(s, axis=-1).astype(v.dtype)
    return jnp.einsum('bhsm,bmhd->bshd', p, v).reshape(B, S, MEM_WIDTH)


def setup_inputs(seed: int = 0) -> dict:
    key = jax.random.key(seed)
    ks = jax.random.split(key, 20)
    f32 = jnp.float32

    def w(k, shape, fan_in):
        return jax.random.normal(k, shape, f32) * fan_in ** -0.5

    def gain(k, shape):
        return 1.0 + 0.02 * jax.random.normal(k, shape, f32)

    x = jax.random.normal(ks[0], (BATCH, SEQ, D_MODEL), f32)
    mem = jax.random.normal(ks[1], (BATCH, N_MEM, D_MODEL), f32)
    offset = jax.random.randint(ks[2], (BATCH, 1), 0, 1024, dtype=jnp.int32)
    positions = offset + jnp.arange(SEQ, dtype=jnp.int32)[None, :]
    L = DEPTH
    return {
        'x': x,
        'mem': mem,
        'positions': positions,
        'pre_mix_g': gain(ks[3], (L, D_MODEL)),
        'w_in': w(ks[4], (L, D_MODEL, IN_WIDTH), D_MODEL),
        'mla_q_norm_g': gain(ks[5], (L, MLA_Q_RANK)),
        'w_uq': w(ks[6], (L, MLA_Q_RANK, MLA_HEADS * (MLA_NOPE + MLA_ROPE)), MLA_Q_RANK),
        'mla_kv_norm_g': gain(ks[7], (L, MLA_KV_RANK)),
        'w_ukv': w(ks[8], (L, MLA_KV_RANK, MLA_HEADS * (MLA_NOPE + MLA_V)), MLA_KV_RANK),
        'mem_norm_g': gain(ks[9], (L, D_MODEL)),
        'w_mem_kv': w(ks[10], (L, D_MODEL, 2 * MEM_WIDTH), D_MODEL),
        'w_out': w(ks[11], (L, MIX_WIDTH, D_MODEL), MIX_WIDTH),
        'post_mix_g': gain(ks[12], (L, D_MODEL)),
        'pre_ffn_g': gain(ks[13], (L, D_MODEL)),
        'w_gate': w(ks[14], (L, D_MODEL, D_FF), D_MODEL),
        'w_up': w(ks[15], (L, D_MODEL, D_FF), D_MODEL),
        'w_down': w(ks[16], (L, D_FF, D_MODEL), D_FF),
        'post_ffn_g': gain(ks[17], (L, D_MODEL)),
    }


def reference(x, mem, positions, pre_mix_g, w_in, mla_q_norm_g, w_uq, mla_kv_norm_g, w_ukv, mem_norm_g, w_mem_kv, w_out, post_mix_g, pre_ffn_g, w_gate, w_up, w_down, post_ffn_g):
    for l in range(DEPTH):
        h = rmsnorm(x, pre_mix_g[l])
        rq, rk, rv, rg, cq, ckv, kr, mq = split_cols(h @ w_in[l])
        ret = retention_group(rq, rk, rv, rg, positions)
        att = mla_group(cq, ckv, kr, mla_q_norm_g[l], w_uq[l], mla_kv_norm_g[l], w_ukv[l], positions)
        mo = memory_group(mq, mem, mem_norm_g[l], w_mem_kv[l])
        mixed = jnp.concatenate([ret, att.astype(ret.dtype), mo.astype(ret.dtype)], axis=-1) @ w_out[l]
        x = x + rmsnorm(mixed, post_mix_g[l])
        h = rmsnorm(x, pre_ffn_g[l])
        f = (jax.nn.silu(h @ w_gate[l]) * (h @ w_up[l])) @ w_down[l]
        x = x + rmsnorm(f, post_ffn_g[l])
    return x
```

```python
import functools
import math

import jax
import jax.numpy as jnp
import numpy as np
from jax import lax
from jax.experimental import pallas as pl
from jax.experimental.pallas import tpu as pltpu

D_MODEL = 1024
SEQ = 16384
DEPTH = 4
N_MEM = 256
RET_HEADS = 4
RET_HEAD_DIM = 128
RET_WIDTH = RET_HEADS * RET_HEAD_DIM
RET_CHUNK = 128
MLA_HEADS = 4
MLA_NOPE = 128
MLA_ROPE = 64
MLA_V = 64
MLA_Q_RANK = 256
MLA_KV_RANK = 128
MLA_WIDTH = MLA_HEADS * MLA_V
MEM_HEADS = 4
MEM_HEAD_DIM = 64
MEM_WIDTH = MEM_HEADS * MEM_HEAD_DIM
D_FF = 2816
ROPE_BASE = 10000.0
EPS = 1e-6

LANES = 128
MLA_QK = MLA_NOPE + LANES
MLA_VW = LANES

_C_RQ, _C_RK, _C_RV, _C_RG = 0, 512, 1024, 1536
_C_CQ = 2048
_C_CKV = _C_CQ + MLA_Q_RANK
_C_KR = _C_CKV + MLA_KV_RANK
_C_MQ = _C_KR + LANES
IN_COLS = _C_MQ + MEM_WIDTH

TM_PROJ = 512
TM_RET = 512
T_ATT = 512
FF_CHUNK = 256
VMEM_LIMIT = 56 * 1024 * 1024

f32 = jnp.float32
bf16 = jnp.bfloat16
NEG = float(np.finfo(np.float32).min)


def _nt_dot(a, b):
    return lax.dot_general(a, b, (((1,), (1,)), ((), ())), preferred_element_type=f32)


def _tn_dot(a, b):
    return lax.dot_general(a, b, (((0,), (0,)), ((), ())), preferred_element_type=f32)


def _dot(a, b):
    return jnp.dot(a, b, preferred_element_type=f32)


def _rms(x):
    return x * lax.rsqrt(jnp.mean(x * x, axis=-1, keepdims=True) + EPS)


def _silu(x):
    return x * (1.0 / (1.0 + jnp.exp(-x)))


def _const(shape):
    nd = len(shape)
    return pl.BlockSpec(shape, lambda *_: (0,) * nd)


def _params(sem):
    return pltpu.CompilerParams(dimension_semantics=sem, vmem_limit_bytes=VMEM_LIMIT)


def _rope_table_kernel(pos_ref, inv_ref, csel_ref, ssel_ref, cos_ref, sin_ref):
    pos = pos_ref[...].astype(f32)
    for t in range(2):
        ang = pos * inv_ref[t:t + 1, :]
        cos_ref[t] = jnp.cos(ang) * csel_ref[t:t + 1, :]
        sin_ref[t] = jnp.sin(ang) * ssel_ref[t:t + 1, :]


def _rope_tables(positions):
    half_r = RET_HEAD_DIM // 2
    half_m = MLA_ROPE // 2
    inv_r = ROPE_BASE ** (-jnp.arange(half_r, dtype=f32) / half_r)
    inv_m = ROPE_BASE ** (-jnp.arange(half_m, dtype=f32) / half_m)
    zeros = jnp.zeros((half_m,), f32)
    ones = jnp.ones((half_m,), f32)
    inv = jnp.stack([jnp.concatenate([inv_r, inv_r]), jnp.concatenate([inv_m, zeros, inv_m, zeros])])
    csel = jnp.stack([jnp.ones((LANES,), f32), jnp.concatenate([ones, zeros, ones, zeros])])
    ssel = jnp.stack([jnp.concatenate([-jnp.ones((half_r,), f32), jnp.ones((half_r,), f32)]),
                      jnp.concatenate([-ones, zeros, ones, zeros])])
    tm = 2048
    tab = jax.ShapeDtypeStruct((2, SEQ, LANES), f32)
    return pl.pallas_call(
        _rope_table_kernel,
        grid=(SEQ // tm,),
        in_specs=[pl.BlockSpec((tm, 1), lambda i: (i, 0)), _const((2, LANES)), _const((2, LANES)),
                  _const((2, LANES))],
        out_specs=[pl.BlockSpec((2, tm, LANES), lambda i: (0, i, 0))] * 2,
        out_shape=[tab, tab],
        compiler_params=_params(("parallel",)),
    )(positions.reshape(SEQ, 1), inv, csel, ssel)


def _rope(t, c, s):
    return t * c + pltpu.roll(t, LANES // 2, 1) * s


def _inproj_kernel(x_ref, g_ref, w_ref, cos_ref, sin_ref, qg_ref, wuq_ref, kvg_ref, wukv_ref,
                   rq_o, rk_o, rv_o, rg_o, qm_o, km_o, vm_o, mq_o):
    h = (_rms(x_ref[...]) * g_ref[...]).astype(bf16)
    cr, sr = cos_ref[0], sin_ref[0]
    cm, sm = cos_ref[1], sin_ref[1]

    def proj(lo, width):
        return _dot(h, w_ref[:, lo:lo + width])

    ret_scale = RET_HEAD_DIM ** -0.5
    rq = proj(_C_RQ, RET_WIDTH)
    rk = proj(_C_RK, RET_WIDTH)
    for hd in range(RET_HEADS):
        sl = slice(hd * LANES, (hd + 1) * LANES)
        rq_o[:, sl] = (_rope(rq[:, sl], cr, sr) * ret_scale).astype(bf16)
        rk_o[:, sl] = _rope(rk[:, sl], cr, sr).astype(bf16)
    rv_o[...] = proj(_C_RV, RET_WIDTH).astype(bf16)
    rg_o[...] = _silu(proj(_C_RG, RET_WIDTH))

    att_scale = (MLA_NOPE + MLA_ROPE) ** -0.5
    cq = (_rms(proj(_C_CQ, MLA_Q_RANK)) * qg_ref[...]).astype(bf16)
    q = _dot(cq, wuq_ref[...])
    ckv = (_rms(proj(_C_CKV, MLA_KV_RANK)) * kvg_ref[...]).astype(bf16)
    kv = _dot(ckv, wukv_ref[...])
    kr = _rope(proj(_C_KR, LANES), cm, sm).astype(bf16)
    ones_hi = (lax.broadcasted_iota(jnp.int32, (1, LANES), 1) >= MLA_V).astype(f32)
    for hd in range(MLA_HEADS):
        qn = q[:, hd * MLA_QK:hd * MLA_QK + MLA_NOPE]
        qr = q[:, hd * MLA_QK + MLA_NOPE:(hd + 1) * MLA_QK]
        qm_o[hd, :, :MLA_NOPE] = (qn * att_scale).astype(bf16)
        qm_o[hd, :, MLA_NOPE:] = (_rope(qr, cm, sm) * att_scale).astype(bf16)
        km_o[hd, :, :MLA_NOPE] = kv[:, hd * MLA_NOPE:(hd + 1) * MLA_NOPE].astype(bf16)
        km_o[hd, :, MLA_NOPE:] = kr
        v_lo = MLA_HEADS * MLA_NOPE + hd * MLA_VW
        vm_o[hd] = (kv[:, v_lo:v_lo + MLA_VW] + ones_hi).astype(bf16)
    mq_o[...] = (proj(_C_MQ, MEM_WIDTH) * (MEM_HEAD_DIM ** -0.5)).astype(bf16)


def _inproj(x, g, w_in, cos, sin, qg, wuq, kvg, wukv):
    tm = TM_PROJ
    row = lambda width: pl.BlockSpec((tm, width), lambda i: (i, 0))
    headed = lambda width: pl.BlockSpec((MLA_HEADS, tm, width), lambda i: (0, i, 0))
    sds = jax.ShapeDtypeStruct
    return pl.pallas_call(
        _inproj_kernel,
        grid=(SEQ // tm,),
        in_specs=[row(D_MODEL), _const((1, D_MODEL)), _const((D_MODEL, IN_COLS)),
                  pl.BlockSpec((2, tm, LANES), lambda i: (0, i, 0)),
                  pl.BlockSpec((2, tm, LANES), lambda i: (0, i, 0)),
                  _const((1, MLA_Q_RANK)), _const((MLA_Q_RANK, MLA_HEADS * MLA_QK)),
                  _const((1, MLA_KV_RANK)), _const((MLA_KV_RANK, MLA_HEADS * (MLA_NOPE + MLA_VW)))],
        out_specs=[row(RET_WIDTH), row(RET_WIDTH), row(RET_WIDTH), row(RET_WIDTH),
                   headed(MLA_QK), headed(MLA_QK), headed(MLA_VW), row(MEM_WIDTH)],
        out_shape=[sds((SEQ, RET_WIDTH), bf16), sds((SEQ, RET_WIDTH), bf16), sds((SEQ, RET_WIDTH), bf16),
                   sds((SEQ, RET_WIDTH), f32),
                   sds((MLA_HEADS, SEQ, MLA_QK), bf16), sds((MLA_HEADS, SEQ, MLA_QK), bf16),
                   sds((MLA_HEADS, SEQ, MLA_VW), bf16), sds((SEQ, MEM_WIDTH), bf16)],
        compiler_params=_params(("parallel",)),
    )(x, g, w_in, cos, sin, qg, wuq, kvg, wukv)


def _log_gamma(hd):
    return math.log(1.0 - 2.0 ** (-5.0 - hd))


def _retention_kernel(q_ref, k_ref, v_ref, g_ref, o_ref, state, decay, zeta, xi):
    C = RET_CHUNK

    @pl.when(pl.program_id(0) == 0)
    def _():
        state[...] = jnp.zeros_like(state)
        r = lax.broadcasted_iota(jnp.int32, (C, C), 0).astype(f32)
        c = lax.broadcasted_iota(jnp.int32, (C, C), 1).astype(f32)
        diff = r - c
        for hd in range(RET_HEADS):
            lg = _log_gamma(hd)
            decay[hd] = jnp.where(diff >= 0, jnp.exp(lg * jnp.maximum(diff, 0.0)), 0.0)
            zeta[hd] = jnp.exp(lg * (C - 1.0 - r))
            xi[hd] = jnp.exp(lg * (r + 1.0))

    for ch in range(TM_RET // C):
        rows = slice(ch * C, (ch + 1) * C)
        for hd in range(RET_HEADS):
            cols = slice(hd * RET_HEAD_DIM, (hd + 1) * RET_HEAD_DIM)
            q = q_ref[rows, cols]
            k = k_ref[rows, cols]
            v = v_ref[rows, cols]
            st = state[hd]
            scores = _nt_dot(q, k) * decay[hd]
            o = _dot(scores.astype(bf16), v) + _dot(q, st.astype(bf16)) * xi[hd]
            kz = (k.astype(f32) * zeta[hd]).astype(bf16)
            state[hd] = math.exp(_log_gamma(hd) * C) * st + _tn_dot(kz, v)
            o_ref[rows, cols] = (g_ref[rows, cols] * _rms(o)).astype(bf16)


def _retention(rq, rk, rv, rg):
    tm = TM_RET
    row = pl.BlockSpec((tm, RET_WIDTH), lambda i: (i, 0))
    tab = pltpu.VMEM((RET_HEADS, RET_CHUNK, RET_CHUNK), f32)
    return pl.pallas_call(
        _retention_kernel,
        grid=(SEQ // tm,),
        in_specs=[row, row, row, row],
        out_specs=row,
        out_shape=jax.ShapeDtypeStruct((SEQ, RET_WIDTH), bf16),
        scratch_shapes=[pltpu.VMEM((RET_HEADS, RET_HEAD_DIM, RET_HEAD_DIM), f32), tab, tab, tab],
        compiler_params=_params(("arbitrary",)),
    )(rq, rk, rv, rg)


def _mla_kernel(q_ref, k_ref, v_ref, o_ref, m_sc, acc_sc):
    T = T_ATT
    qi = pl.program_id(1)
    q = q_ref[...]
    m_sc[...] = jnp.full_like(m_sc, NEG)
    acc_sc[...] = jnp.zeros_like(acc_sc)

    def update(s, v):
        m_prev = m_sc[...]
        m_new = jnp.maximum(m_prev, jnp.max(s, axis=-1, keepdims=True))
        p = jnp.exp(s - m_new).astype(bf16)
        acc_sc[...] = acc_sc[...] * jnp.exp(m_prev - m_new) + _dot(p, v)
        m_sc[...] = m_new

    def body(j, carry):
        off = pl.multiple_of(j * T, T)
        update(_nt_dot(q, k_ref[pl.ds(off, T), :]), v_ref[pl.ds(off, T), :])
        return carry

    lax.fori_loop(0, qi, body, 0)

    off = pl.multiple_of(qi * T, T)
    s = _nt_dot(q, k_ref[pl.ds(off, T), :])
    r = lax.broadcasted_iota(jnp.int32, (T, T), 0)
    c = lax.broadcasted_iota(jnp.int32, (T, T), 1)
    update(jnp.where(c <= r, s, NEG), v_ref[pl.ds(off, T), :])

    acc = acc_sc[...]
    o_ref[...] = (acc[:, :MLA_V] / acc[:, MLA_V:]).astype(bf16)


def _mla_attention(qm, km, vm):
    T = T_ATT
    return pl.pallas_call(
        _mla_kernel,
        grid=(MLA_HEADS, SEQ // T),
        in_specs=[pl.BlockSpec((None, T, MLA_QK), lambda h, i: (h, i, 0)),
                  pl.BlockSpec((None, SEQ, MLA_QK), lambda h, i: (h, 0, 0)),
                  pl.BlockSpec((None, SEQ, MLA_VW), lambda h, i: (h, 0, 0))],
        out_specs=pl.BlockSpec((None, T, MLA_V), lambda h, i: (h, i, 0)),
        out_shape=jax.ShapeDtypeStruct((MLA_HEADS, SEQ, MLA_V), bf16),
        scratch_shapes=[pltpu.VMEM((T, 1), f32), pltpu.VMEM((T, MLA_VW), f32)],
        compiler_params=_params(("arbitrary", "arbitrary")),
    )(qm, km, vm)


def _mix_kernel(x_ref, ret_ref, att_ref, mq_ref, mem_ref, memg_ref, wmem_ref, wo_ref, g_ref, o_ref, kv_sc):
    @pl.when(pl.program_id(0) == 0)
    def _():
        mn = (_rms(mem_ref[...]) * memg_ref[...]).astype(bf16)
        kv_sc[...] = _dot(mn, wmem_ref[...]).astype(bf16)

    mixed = _dot(ret_ref[...], wo_ref[:RET_WIDTH, :])
    for hd in range(MLA_HEADS):
        lo = RET_WIDTH + hd * MLA_V
        mixed += _dot(att_ref[hd], wo_ref[lo:lo + MLA_V, :])
    mq = mq_ref[...]
    for hd in range(MEM_HEADS):
        cols = slice(hd * MEM_HEAD_DIM, (hd + 1) * MEM_HEAD_DIM)
        k = kv_sc[:, cols]
        v = kv_sc[:, MEM_WIDTH + hd * MEM_HEAD_DIM:MEM_WIDTH + (hd + 1) * MEM_HEAD_DIM]
        s = _nt_dot(mq[:, cols], k)
        p = jnp.exp(s - jnp.max(s, axis=-1, keepdims=True))
        mo = _dot(p.astype(bf16), v) / jnp.sum(p, axis=-1, keepdims=True)
        lo = RET_WIDTH + MLA_WIDTH + hd * MEM_HEAD_DIM
        mixed += _dot(mo.astype(bf16), wo_ref[lo:lo + MEM_HEAD_DIM, :])
    o_ref[...] = x_ref[...] + _rms(mixed) * g_ref[...]


def _mix(x, ret, att, mq, mem, memg, wmem, wo, g):
    tm = TM_PROJ
    row = lambda width: pl.BlockSpec((tm, width), lambda i: (i, 0))
    return pl.pallas_call(
        _mix_kernel,
        grid=(SEQ // tm,),
        in_specs=[row(D_MODEL), row(RET_WIDTH), pl.BlockSpec((MLA_HEADS, tm, MLA_V), lambda i: (0, i, 0)),
                  row(MEM_WIDTH), _const((N_MEM, D_MODEL)), _const((1, D_MODEL)),
                  _const((D_MODEL, 2 * MEM_WIDTH)), _const((D_MODEL, D_MODEL)), _const((1, D_MODEL))],
        out_specs=row(D_MODEL),
        out_shape=jax.ShapeDtypeStruct((SEQ, D_MODEL), f32),
        scratch_shapes=[pltpu.VMEM((N_MEM, 2 * MEM_WIDTH), bf16)],
        compiler_params=_params(("arbitrary",)),
    )(x, ret, att, mq, mem, memg, wmem, wo, g)


def _ffn_kernel(x_ref, gin_ref, wg_ref, wu_ref, wd_ref, gout_ref, o_ref):
    x = x_ref[...]
    h = (_rms(x) * gin_ref[...]).astype(bf16)
    f = jnp.zeros((x.shape[0], D_MODEL), f32)
    for c in range(D_FF // FF_CHUNK):
        cols = slice(c * FF_CHUNK, (c + 1) * FF_CHUNK)
        a = _silu(_dot(h, wg_ref[:, cols])) * _dot(h, wu_ref[:, cols])
        f += _dot(a.astype(bf16), wd_ref[cols, :])
    o_ref[...] = x + _rms(f) * gout_ref[...]


def _ffn(x, gin, wg, wu, wd, gout):
    tm = TM_PROJ
    row = pl.BlockSpec((tm, D_MODEL), lambda i: (i, 0))
    return pl.pallas_call(
        _ffn_kernel,
        grid=(SEQ // tm,),
        in_specs=[row, _const((1, D_MODEL)), _const((D_MODEL, D_FF)), _const((D_MODEL, D_FF)),
                  _const((D_FF, D_MODEL)), _const((1, D_MODEL))],
        out_specs=row,
        out_shape=jax.ShapeDtypeStruct((SEQ, D_MODEL), f32),
        compiler_params=_params(("parallel",)),
    )(x, gin, wg, wu, wd, gout)


def _layout_w_in(w):
    half = MLA_ROPE // 2
    base = 4 * RET_WIDTH
    cq = w[:, base:base + MLA_Q_RANK]
    ckv = w[:, base + MLA_Q_RANK:base + MLA_Q_RANK + MLA_KV_RANK]
    kr0 = base + MLA_Q_RANK + MLA_KV_RANK
    kr = w[:, kr0:kr0 + MLA_ROPE]
    mq = w[:, kr0 + MLA_ROPE:]
    z = jnp.zeros((w.shape[0], half), w.dtype)
    slab = jnp.concatenate([kr[:, :half], z, kr[:, half:], z], axis=1)
    return jnp.concatenate([w[:, :base], cq, ckv, slab, mq], axis=1).astype(bf16)


def _layout_w_uq(w):
    half = MLA_ROPE // 2
    w = w.reshape(MLA_Q_RANK, MLA_HEADS, MLA_NOPE + MLA_ROPE)
    z = jnp.zeros((MLA_Q_RANK, MLA_HEADS, half), w.dtype)
    x1 = w[..., MLA_NOPE:MLA_NOPE + half]
    x2 = w[..., MLA_NOPE + half:]
    return jnp.concatenate([w[..., :MLA_NOPE], x1, z, x2, z], axis=-1).reshape(MLA_Q_RANK, -1).astype(bf16)


def _layout_w_ukv(w):
    w = w.reshape(MLA_KV_RANK, MLA_HEADS, MLA_NOPE + MLA_V)
    kn = w[..., :MLA_NOPE].reshape(MLA_KV_RANK, -1)
    v = jnp.concatenate([w[..., MLA_NOPE:], jnp.zeros((MLA_KV_RANK, MLA_HEADS, MLA_VW - MLA_V), w.dtype)], axis=-1)
    return jnp.concatenate([kn, v.reshape(MLA_KV_RANK, -1)], axis=1).astype(bf16)


def kernel(x, mem, positions, pre_mix_g, w_in, mla_q_norm_g, w_uq, mla_kv_norm_g, w_ukv, mem_norm_g, w_mem_kv,
           w_out, post_mix_g, pre_ffn_g, w_gate, w_up, w_down, post_ffn_g):
    assert x.shape == (1, SEQ, D_MODEL) and mem.shape == (1, N_MEM, D_MODEL)
    xs = x[0]
    mem2 = mem[0]
    cos, sin = _rope_tables(positions)
    vec = lambda g, l: g[l][None, :]
    for l in range(DEPTH):
        rq, rk, rv, rg, qm, km, vm, mq = _inproj(
            xs, vec(pre_mix_g, l), _layout_w_in(w_in[l]), cos, sin,
            vec(mla_q_norm_g, l), _layout_w_uq(w_uq[l]), vec(mla_kv_norm_g, l), _layout_w_ukv(w_ukv[l]))
        ret = _retention(rq, rk, rv, rg)
        att = _mla_attention(qm, km, vm)
        xs = _mix(xs, ret, att, mq, mem2, vec(mem_norm_g, l), w_mem_kv[l].astype(bf16), w_out[l].astype(bf16),
                  vec(post_mix_g, l))
        xs = _ffn(xs, vec(pre_ffn_g, l), w_gate[l].astype(bf16), w_up[l].astype(bf16), w_down[l].astype(bf16),
                  vec(post_ffn_g, l))
    return xs[None]
```

```python
import functools
import math

import jax
import jax.numpy as jnp
import numpy as np
from jax import lax
from jax.experimental import pallas as pl
from jax.experimental.pallas import tpu as pltpu

D_MODEL = 1024
SEQ = 16384
DEPTH = 4
N_MEM = 256
RET_HEADS = 4
RET_HEAD_DIM = 128
RET_WIDTH = RET_HEADS * RET_HEAD_DIM
RET_CHUNK = 128
MLA_HEADS = 4
MLA_NOPE = 128
MLA_ROPE = 64
MLA_V = 64
MLA_Q_RANK = 256
MLA_KV_RANK = 128
MLA_WIDTH = MLA_HEADS * MLA_V
MEM_HEADS = 4
MEM_HEAD_DIM = 64
MEM_WIDTH = MEM_HEADS * MEM_HEAD_DIM
D_FF = 2816
ROPE_BASE = 10000.0
EPS = 1e-6

LANES = 128
MLA_QK = MLA_NOPE + LANES
MLA_VW = LANES

_C_RQ, _C_RK, _C_RV, _C_RG = 0, 512, 1024, 1536
_C_CQ = 2048
_C_CKV = _C_CQ + MLA_Q_RANK
_C_KR = _C_CKV + MLA_KV_RANK
_C_MQ = _C_KR + LANES
IN_COLS = _C_MQ + MEM_WIDTH

TM_PROJ = 512
TM_RET = 512
T_ATT = 2048
ATT_KEYS = 1024
ATT_ROWS = 512
LOG2E = math.log2(math.e)
FF_CHUNK = 256
VMEM_LIMIT = 56 * 1024 * 1024

f32 = jnp.float32
bf16 = jnp.bfloat16
NEG = float(np.finfo(np.float32).min)


def _nt_dot(a, b):
    return lax.dot_general(a, b, (((1,), (1,)), ((), ())), preferred_element_type=f32)


def _tn_dot(a, b):
    return lax.dot_general(a, b, (((0,), (0,)), ((), ())), preferred_element_type=f32)


def _dot(a, b):
    return jnp.dot(a, b, preferred_element_type=f32)


def _rms(x):
    return x * lax.rsqrt(jnp.mean(x * x, axis=-1, keepdims=True) + EPS)


def _silu(x):
    return x * (1.0 / (1.0 + jnp.exp(-x)))


def _const(shape):
    nd = len(shape)
    return pl.BlockSpec(shape, lambda *_: (0,) * nd)


def _params(sem):
    return pltpu.CompilerParams(dimension_semantics=sem, vmem_limit_bytes=VMEM_LIMIT)


def _rope_table_kernel(pos_ref, inv_ref, csel_ref, ssel_ref, cos_ref, sin_ref):
    pos = pos_ref[...].astype(f32)
    for t in range(2):
        ang = pos * inv_ref[t:t + 1, :]
        cos_ref[t] = jnp.cos(ang) * csel_ref[t:t + 1, :]
        sin_ref[t] = jnp.sin(ang) * ssel_ref[t:t + 1, :]


def _rope_tables(positions):
    half_r = RET_HEAD_DIM // 2
    half_m = MLA_ROPE // 2
    inv_r = ROPE_BASE ** (-jnp.arange(half_r, dtype=f32) / half_r)
    inv_m = ROPE_BASE ** (-jnp.arange(half_m, dtype=f32) / half_m)
    zeros = jnp.zeros((half_m,), f32)
    ones = jnp.ones((half_m,), f32)
    inv = jnp.stack([jnp.concatenate([inv_r, inv_r]), jnp.concatenate([inv_m, zeros, inv_m, zeros])])
    csel = jnp.stack([jnp.ones((LANES,), f32), jnp.concatenate([ones, zeros, ones, zeros])])
    ssel = jnp.stack([jnp.concatenate([-jnp.ones((half_r,), f32), jnp.ones((half_r,), f32)]),
                      jnp.concatenate([-ones, zeros, ones, zeros])])
    tm = 2048
    tab = jax.ShapeDtypeStruct((2, SEQ, LANES), f32)
    return pl.pallas_call(
        _rope_table_kernel,
        grid=(SEQ // tm,),
        in_specs=[pl.BlockSpec((tm, 1), lambda i: (i, 0)), _const((2, LANES)), _const((2, LANES)),
                  _const((2, LANES))],
        out_specs=[pl.BlockSpec((2, tm, LANES), lambda i: (0, i, 0))] * 2,
        out_shape=[tab, tab],
        compiler_params=_params(("parallel",)),
        name="rope_tables",
    )(positions.reshape(SEQ, 1), inv, csel, ssel)


def _rope(t, c, s):
    return t * c + pltpu.roll(t, LANES // 2, 1) * s


def _inproj_kernel(x_ref, g_ref, w_ref, cos_ref, sin_ref, qg_ref, wuq_ref, kvg_ref, wukv_ref,
                   rq_o, rk_o, rv_o, rg_o, qm_o, km_o, vm_o, mq_o):
    h = (_rms(x_ref[...]) * g_ref[...]).astype(bf16)
    cr, sr = cos_ref[0], sin_ref[0]
    cm, sm = cos_ref[1], sin_ref[1]

    def proj(lo, width):
        return _dot(h, w_ref[:, lo:lo + width])

    ret_scale = RET_HEAD_DIM ** -0.5
    rq = proj(_C_RQ, RET_WIDTH)
    rk = proj(_C_RK, RET_WIDTH)
    for hd in range(RET_HEADS):
        sl = slice(hd * LANES, (hd + 1) * LANES)
        rq_o[:, sl] = (_rope(rq[:, sl], cr, sr) * ret_scale).astype(bf16)
        rk_o[:, sl] = _rope(rk[:, sl], cr, sr).astype(bf16)
    rv_o[...] = proj(_C_RV, RET_WIDTH).astype(bf16)
    rg_o[...] = _silu(proj(_C_RG, RET_WIDTH))

    att_scale = (MLA_NOPE + MLA_ROPE) ** -0.5 * LOG2E
    cq = (_rms(proj(_C_CQ, MLA_Q_RANK)) * qg_ref[...]).astype(bf16)
    q = _dot(cq, wuq_ref[...])
    ckv = (_rms(proj(_C_CKV, MLA_KV_RANK)) * kvg_ref[...]).astype(bf16)
    kv = _dot(ckv, wukv_ref[...])
    kr = _rope(proj(_C_KR, LANES), cm, sm).astype(bf16)
    ones_hi = (lax.broadcasted_iota(jnp.int32, (1, LANES), 1) >= MLA_V).astype(f32)
    for hd in range(MLA_HEADS):
        qn = q[:, hd * MLA_QK:hd * MLA_QK + MLA_NOPE]
        qr = q[:, hd * MLA_QK + MLA_NOPE:(hd + 1) * MLA_QK]
        qm_o[hd, :, :MLA_NOPE] = (qn * att_scale).astype(bf16)
        qm_o[hd, :, MLA_NOPE:] = (_rope(qr, cm, sm) * att_scale).astype(bf16)
        km_o[hd, :, :MLA_NOPE] = kv[:, hd * MLA_NOPE:(hd + 1) * MLA_NOPE].astype(bf16)
        km_o[hd, :, MLA_NOPE:] = kr
        v_lo = MLA_HEADS * MLA_NOPE + hd * MLA_VW
        vm_o[hd] = (kv[:, v_lo:v_lo + MLA_VW] + ones_hi).astype(bf16)
    mq_o[...] = (proj(_C_MQ, MEM_WIDTH) * (MEM_HEAD_DIM ** -0.5)).astype(bf16)


def _inproj(x, g, w_in, cos, sin, qg, wuq, kvg, wukv):
    tm = TM_PROJ
    row = lambda width: pl.BlockSpec((tm, width), lambda i: (i, 0))
    headed = lambda width: pl.BlockSpec((MLA_HEADS, tm, width), lambda i: (0, i, 0))
    sds = jax.ShapeDtypeStruct
    return pl.pallas_call(
        _inproj_kernel,
        grid=(SEQ // tm,),
        in_specs=[row(D_MODEL), _const((1, D_MODEL)), _const((D_MODEL, IN_COLS)),
                  pl.BlockSpec((2, tm, LANES), lambda i: (0, i, 0)),
                  pl.BlockSpec((2, tm, LANES), lambda i: (0, i, 0)),
                  _const((1, MLA_Q_RANK)), _const((MLA_Q_RANK, MLA_HEADS * MLA_QK)),
                  _const((1, MLA_KV_RANK)), _const((MLA_KV_RANK, MLA_HEADS * (MLA_NOPE + MLA_VW)))],
        out_specs=[row(RET_WIDTH), row(RET_WIDTH), row(RET_WIDTH), row(RET_WIDTH),
                   headed(MLA_QK), headed(MLA_QK), headed(MLA_VW), row(MEM_WIDTH)],
        out_shape=[sds((SEQ, RET_WIDTH), bf16), sds((SEQ, RET_WIDTH), bf16), sds((SEQ, RET_WIDTH), bf16),
                   sds((SEQ, RET_WIDTH), f32),
                   sds((MLA_HEADS, SEQ, MLA_QK), bf16), sds((MLA_HEADS, SEQ, MLA_QK), bf16),
                   sds((MLA_HEADS, SEQ, MLA_VW), bf16), sds((SEQ, MEM_WIDTH), bf16)],
        compiler_params=_params(("parallel",)),
        name="in_proj",
    )(x, g, w_in, cos, sin, qg, wuq, kvg, wukv)


def _log_gamma(hd):
    return math.log(1.0 - 2.0 ** (-5.0 - hd))


def _retention_kernel(q_ref, k_ref, v_ref, g_ref, o_ref, state, decay, zeta, xi):
    C = RET_CHUNK

    @pl.when(pl.program_id(0) == 0)
    def _():
        state[...] = jnp.zeros_like(state)
        r = lax.broadcasted_iota(jnp.int32, (C, C), 0).astype(f32)
        c = lax.broadcasted_iota(jnp.int32, (C, C), 1).astype(f32)
        diff = r - c
        for hd in range(RET_HEADS):
            lg = _log_gamma(hd)
            decay[hd] = jnp.where(diff >= 0, jnp.exp(lg * jnp.maximum(diff, 0.0)), 0.0)
            zeta[hd] = jnp.exp(lg * (C - 1.0 - r))
            xi[hd] = jnp.exp(lg * (r + 1.0))

    for ch in range(TM_RET // C):
        rows = slice(ch * C, (ch + 1) * C)
        for hd in range(RET_HEADS):
            cols = slice(hd * RET_HEAD_DIM, (hd + 1) * RET_HEAD_DIM)
            q = q_ref[rows, cols]
            k = k_ref[rows, cols]
            v = v_ref[rows, cols]
            st = state[hd]
            scores = _nt_dot(q, k) * decay[hd]
            o = _dot(scores.astype(bf16), v) + _dot(q, st.astype(bf16)) * xi[hd]
            kz = (k.astype(f32) * zeta[hd]).astype(bf16)
            state[hd] = math.exp(_log_gamma(hd) * C) * st + _tn_dot(kz, v)
            o_ref[rows, cols] = (g_ref[rows, cols] * _rms(o)).astype(bf16)


def _retention(rq, rk, rv, rg):
    tm = TM_RET
    row = pl.BlockSpec((tm, RET_WIDTH), lambda i: (i, 0))
    tab = pltpu.VMEM((RET_HEADS, RET_CHUNK, RET_CHUNK), f32)
    return pl.pallas_call(
        _retention_kernel,
        grid=(SEQ // tm,),
        in_specs=[row, row, row, row],
        out_specs=row,
        out_shape=jax.ShapeDtypeStruct((SEQ, RET_WIDTH), bf16),
        scratch_shapes=[pltpu.VMEM((RET_HEADS, RET_HEAD_DIM, RET_HEAD_DIM), f32), tab, tab, tab],
        compiler_params=_params(("arbitrary",)),
        name="retention",
    )(rq, rk, rv, rg)


def _mla_kernel(q_ref, k_ref, v_ref, o_ref, *scratch):
    T, TK, R, NCH = T_ATT, ATT_KEYS, ATT_ROWS, T_ATT // ATT_ROWS
    m_scs, acc_scs = scratch[:NCH], scratch[NCH:]
    qi = pl.program_id(1)
    for c in range(NCH):
        m_scs[c][...] = jnp.full_like(m_scs[c], NEG)
        acc_scs[c][...] = jnp.zeros_like(acc_scs[c])

    def update(c, s, v):
        m_prev = m_scs[c][...]
        m_new = jnp.maximum(m_prev, jnp.max(s, axis=-1, keepdims=True))
        p = jnp.exp2(s - m_new).astype(bf16)
        acc_scs[c][...] = acc_scs[c][...] * jnp.exp2(m_prev - m_new) + _dot(p, v)
        m_scs[c][...] = m_new

    def scores(c, off, width):
        return _nt_dot(q_ref[c * R:(c + 1) * R, :], k_ref[pl.ds(off, width), :])

    def sweep(off, widths, masked):
        s_next = scores(0, off, widths[0])
        for c in range(NCH):
            s, w = s_next, widths[c]
            if c + 1 < NCH:
                s_next = scores(c + 1, off, widths[c + 1])
            if masked:
                row = lax.broadcasted_iota(jnp.int32, (R, w), 0) + c * R
                col = lax.broadcasted_iota(jnp.int32, (R, w), 1)
                s = jnp.where(col <= row, s, NEG)
            update(c, s, v_ref[pl.ds(off, w), :])

    def body(j, carry):
        sweep(pl.multiple_of(j * TK, TK), [TK] * NCH, False)
        return carry

    lax.fori_loop(0, qi * (T // TK), body, 0)

    sweep(pl.multiple_of(qi * T, T), [(c + 1) * R for c in range(NCH)], True)

    for c in range(NCH):
        acc = acc_scs[c][...]
        o_ref[c * R:(c + 1) * R, :] = (acc[:, :MLA_V] / acc[:, MLA_V:]).astype(bf16)


def _mla_attention(qm, km, vm):
    T, R = T_ATT, ATT_ROWS
    nch = T // R
    return pl.pallas_call(
        _mla_kernel,
        grid=(MLA_HEADS, SEQ // T),
        in_specs=[pl.BlockSpec((None, T, MLA_QK), lambda h, i: (h, i, 0)),
                  pl.BlockSpec((None, SEQ, MLA_QK), lambda h, i: (h, 0, 0)),
                  pl.BlockSpec((None, SEQ, MLA_VW), lambda h, i: (h, 0, 0))],
        out_specs=pl.BlockSpec((None, T, MLA_V), lambda h, i: (h, i, 0)),
        out_shape=jax.ShapeDtypeStruct((MLA_HEADS, SEQ, MLA_V), bf16),
        scratch_shapes=[pltpu.VMEM((R, 1), f32)] * nch + [pltpu.VMEM((R, MLA_VW), f32)] * nch,
        compiler_params=_params(("arbitrary", "arbitrary")),
        name="mla_attention",
    )(qm, km, vm)


def _mix_kernel(x_ref, ret_ref, att_ref, mq_ref, mem_ref, memg_ref, wmem_ref, wo_ref, g_ref, o_ref, kv_sc):
    @pl.when(pl.program_id(0) == 0)
    def _():
        mn = (_rms(mem_ref[...]) * memg_ref[...]).astype(bf16)
        kv_sc[...] = _dot(mn, wmem_ref[...]).astype(bf16)

    mixed = _dot(ret_ref[...], wo_ref[:RET_WIDTH, :])
    for hd in range(MLA_HEADS):
        lo = RET_WIDTH + hd * MLA_V
        mixed += _dot(att_ref[hd], wo_ref[lo:lo + MLA_V, :])
    mq = mq_ref[...]
    for hd in range(MEM_HEADS):
        cols = slice(hd * MEM_HEAD_DIM, (hd + 1) * MEM_HEAD_DIM)
        k = kv_sc[:, cols]
        v = kv_sc[:, MEM_WIDTH + hd * MEM_HEAD_DIM:MEM_WIDTH + (hd + 1) * MEM_HEAD_DIM]
        s = _nt_dot(mq[:, cols], k)
        p = jnp.exp(s - jnp.max(s, axis=-1, keepdims=True))
        mo = _dot(p.astype(bf16), v) / jnp.sum(p, axis=-1, keepdims=True)
        lo = RET_WIDTH + MLA_WIDTH + hd * MEM_HEAD_DIM
        mixed += _dot(mo.astype(bf16), wo_ref[lo:lo + MEM_HEAD_DIM, :])
    o_ref[...] = x_ref[...] + _rms(mixed) * g_ref[...]


def _mix(x, ret, att, mq, mem, memg, wmem, wo, g):
    tm = TM_PROJ
    row = lambda width: pl.BlockSpec((tm, width), lambda i: (i, 0))
    return pl.pallas_call(
        _mix_kernel,
        grid=(SEQ // tm,),
        in_specs=[row(D_MODEL), row(RET_WIDTH), pl.BlockSpec((MLA_HEADS, tm, MLA_V), lambda i: (0, i, 0)),
                  row(MEM_WIDTH), _const((N_MEM, D_MODEL)), _const((1, D_MODEL)),
                  _const((D_MODEL, 2 * MEM_WIDTH)), _const((D_MODEL, D_MODEL)), _const((1, D_MODEL))],
        out_specs=row(D_MODEL),
        out_shape=jax.ShapeDtypeStruct((SEQ, D_MODEL), f32),
        scratch_shapes=[pltpu.VMEM((N_MEM, 2 * MEM_WIDTH), bf16)],
        compiler_params=_params(("arbitrary",)),
        name="mix_out_proj",
    )(x, ret, att, mq, mem, memg, wmem, wo, g)


def _ffn_kernel(x_ref, gin_ref, wg_ref, wu_ref, wd_ref, gout_ref, o_ref):
    x = x_ref[...]
    h = (_rms(x) * gin_ref[...]).astype(bf16)
    f = jnp.zeros((x.shape[0], D_MODEL), f32)
    for c in range(D_FF // FF_CHUNK):
        cols = slice(c * FF_CHUNK, (c + 1) * FF_CHUNK)
        a = _silu(_dot(h, wg_ref[:, cols])) * _dot(h, wu_ref[:, cols])
        f += _dot(a.astype(bf16), wd_ref[cols, :])
    o_ref[...] = x + _rms(f) * gout_ref[...]


def _ffn(x, gin, wg, wu, wd, gout):
    tm = TM_PROJ
    row = pl.BlockSpec((tm, D_MODEL), lambda i: (i, 0))
    return pl.pallas_call(
        _ffn_kernel,
        grid=(SEQ // tm,),
        in_specs=[row, _const((1, D_MODEL)), _const((D_MODEL, D_FF)), _const((D_MODEL, D_FF)),
                  _const((D_FF, D_MODEL)), _const((1, D_MODEL))],
        out_specs=row,
        out_shape=jax.ShapeDtypeStruct((SEQ, D_MODEL), f32),
        compiler_params=_params(("parallel",)),
        name="ffn",
    )(x, gin, wg, wu, wd, gout)


def _layout_w_in(w):
    half = MLA_ROPE // 2
    base = 4 * RET_WIDTH
    cq = w[:, base:base + MLA_Q_RANK]
    ckv = w[:, base + MLA_Q_RANK:base + MLA_Q_RANK + MLA_KV_RANK]
    kr0 = base + MLA_Q_RANK + MLA_KV_RANK
    kr = w[:, kr0:kr0 + MLA_ROPE]
    mq = w[:, kr0 + MLA_ROPE:]
    z = jnp.zeros((w.shape[0], half), w.dtype)
    slab = jnp.concatenate([kr[:, :half], z, kr[:, half:], z], axis=1)
    return jnp.concatenate([w[:, :base], cq, ckv, slab, mq], axis=1).astype(bf16)


def _layout_w_uq(w):
    half = MLA_ROPE // 2
    w = w.reshape(MLA_Q_RANK, MLA_HEADS, MLA_NOPE + MLA_ROPE)
    z = jnp.zeros((MLA_Q_RANK, MLA_HEADS, half), w.dtype)
    x1 = w[..., MLA_NOPE:MLA_NOPE + half]
    x2 = w[..., MLA_NOPE + half:]
    return jnp.concatenate([w[..., :MLA_NOPE], x1, z, x2, z], axis=-1).reshape(MLA_Q_RANK, -1).astype(bf16)


def _layout_w_ukv(w):
    w = w.reshape(MLA_KV_RANK, MLA_HEADS, MLA_NOPE + MLA_V)
    kn = w[..., :MLA_NOPE].reshape(MLA_KV_RANK, -1)
    v = jnp.concatenate([w[..., MLA_NOPE:], jnp.zeros((MLA_KV_RANK, MLA_HEADS, MLA_VW - MLA_V), w.dtype)], axis=-1)
    return jnp.concatenate([kn, v.reshape(MLA_KV_RANK, -1)], axis=1).astype(bf16)


def kernel(x, mem, positions, pre_mix_g, w_in, mla_q_norm_g, w_uq, mla_kv_norm_g, w_ukv, mem_norm_g, w_mem_kv,
           w_out, post_mix_g, pre_ffn_g, w_gate, w_up, w_down, post_ffn_g):
    assert x.shape == (1, SEQ, D_MODEL) and mem.shape == (1, N_MEM, D_MODEL)
    xs = x[0]
    mem2 = mem[0]
    cos, sin = _rope_tables(positions)
    vec = lambda g, l: g[l][None, :]
    for l in range(DEPTH):
        rq, rk, rv, rg, qm, km, vm, mq = _inproj(
            xs, vec(pre_mix_g, l), _layout_w_in(w_in[l]), cos, sin,
            vec(mla_q_norm_g, l), _layout_w_uq(w_uq[l]), vec(mla_kv_norm_g, l), _layout_w_ukv(w_ukv[l]))
        ret = _retention(rq, rk, rv, rg)
        att = _mla_attention(qm, km, vm)
        xs = _mix(xs, ret, att, mq, mem2, vec(mem_norm_g, l), w_mem_kv[l].astype(bf16), w_out[l].astype(bf16),
                  vec(post_mix_g, l))
        xs = _ffn(xs, vec(pre_ffn_g, l), w_gate[l].astype(bf16), w_up[l].astype(bf16), w_down[l].astype(bf16),
                  vec(post_ffn_g, l))
    return xs[None]
```

```python
import functools
import math

import jax
import jax.numpy as jnp
import numpy as np
from jax import lax
from jax.experimental import pallas as pl
from jax.experimental.pallas import tpu as pltpu

D_MODEL = 1024
SEQ = 16384
DEPTH = 4
N_MEM = 256
RET_HEADS = 4
RET_HEAD_DIM = 128
RET_WIDTH = RET_HEADS * RET_HEAD_DIM
RET_CHUNK = 128
MLA_HEADS = 4
MLA_NOPE = 128
MLA_ROPE = 64
MLA_V = 64
MLA_Q_RANK = 256
MLA_KV_RANK = 128
MLA_WIDTH = MLA_HEADS * MLA_V
MEM_HEADS = 4
MEM_HEAD_DIM = 64
MEM_WIDTH = MEM_HEADS * MEM_HEAD_DIM
D_FF = 2816
ROPE_BASE = 10000.0
EPS = 1e-6

LANES = 128
ROW_PACK = 16
MLA_QK = MLA_NOPE + LANES
MLA_VW = LANES

_C_RQ, _C_RK, _C_RV, _C_RG = 0, 512, 1024, 1536
_C_CQ = 2048
_C_CKV = _C_CQ + MLA_Q_RANK
_C_KR = _C_CKV + MLA_KV_RANK
_C_MQ = _C_KR + LANES
IN_COLS = _C_MQ + MEM_WIDTH

TM_PROJ = 512
TM_RET = 512
T_ATT = 2048
ATT_KEYS = 2048
ATT_ROWS = 512
ATT_AHEAD = 2
LOG2E = math.log2(math.e)
FF_CHUNK = 256
VMEM_LIMIT = 56 * 1024 * 1024

f32 = jnp.float32
bf16 = jnp.bfloat16
NEG = float(np.finfo(np.float32).min)


def _nt_dot(a, b):
    return lax.dot_general(a, b, (((1,), (1,)), ((), ())), preferred_element_type=f32)


def _tn_dot(a, b):
    return lax.dot_general(a, b, (((0,), (0,)), ((), ())), preferred_element_type=f32)


def _dot(a, b):
    return jnp.dot(a, b, preferred_element_type=f32)


def _rms(x):
    return x * lax.rsqrt(jnp.mean(x * x, axis=-1, keepdims=True) + EPS)


def _silu(x):
    return x * (1.0 / (1.0 + jnp.exp(-x)))


def _const(shape):
    nd = len(shape)
    return pl.BlockSpec(shape, lambda *_: (0,) * nd)


def _layer(l, tail):
    nd = len(tail)
    return pl.BlockSpec((None,) + tuple(tail), lambda *_: (l,) + (0,) * nd)


def _params(sem):
    return pltpu.CompilerParams(dimension_semantics=sem, vmem_limit_bytes=VMEM_LIMIT)


def _rope_table_kernel(pos_ref, inv_ref, csel_ref, ssel_ref, cos_ref, sin_ref):
    pos = pos_ref[...].astype(f32)
    for t in range(2):
        ang = pos * inv_ref[t:t + 1, :]
        cos_ref[t] = jnp.cos(ang) * csel_ref[t:t + 1, :]
        sin_ref[t] = jnp.sin(ang) * ssel_ref[t:t + 1, :]


def _rope_tables(positions):
    half_r = RET_HEAD_DIM // 2
    half_m = MLA_ROPE // 2
    inv_r = ROPE_BASE ** (-jnp.arange(half_r, dtype=f32) / half_r)
    inv_m = ROPE_BASE ** (-jnp.arange(half_m, dtype=f32) / half_m)
    zeros = jnp.zeros((half_m,), f32)
    ones = jnp.ones((half_m,), f32)
    inv = jnp.stack([jnp.concatenate([inv_r, inv_r]), jnp.concatenate([inv_m, zeros, inv_m, zeros])])
    csel = jnp.stack([jnp.ones((LANES,), f32), jnp.concatenate([ones, zeros, ones, zeros])])
    ssel = jnp.stack([jnp.concatenate([-jnp.ones((half_r,), f32), jnp.ones((half_r,), f32)]),
                      jnp.concatenate([-ones, zeros, ones, zeros])])
    tm = 2048
    tab = jax.ShapeDtypeStruct((2, SEQ, LANES), f32)
    return pl.pallas_call(
        _rope_table_kernel,
        grid=(SEQ // tm,),
        in_specs=[pl.BlockSpec((tm, 1), lambda i: (i, 0)), _const((2, LANES)), _const((2, LANES)),
                  _const((2, LANES))],
        out_specs=[pl.BlockSpec((2, tm, LANES), lambda i: (0, i, 0))] * 2,
        out_shape=[tab, tab],
        compiler_params=_params(("parallel",)),
        name="rope_tables",
    )(positions.reshape(SEQ, 1), inv, csel, ssel)


def _rope(t, c, s):
    return t * c + pltpu.roll(t, LANES // 2, 1) * s


def _inproj_kernel(x_ref, g_ref, w_ref, cos_ref, sin_ref, qg_ref, wuq_ref, kvg_ref, wukv_ref,
                   rq_o, rk_o, rv_o, rg_o, qm_o, km_o, vm_o, mq_o):
    h = (_rms(x_ref[...]) * g_ref[...]).astype(bf16)
    cr, sr = cos_ref[0], sin_ref[0]
    cm, sm = cos_ref[1], sin_ref[1]

    def proj(lo, width):
        return _dot(h, w_ref[:, lo:lo + width])

    cq = (_rms(proj(_C_CQ, MLA_Q_RANK)) * qg_ref[...]).astype(bf16)
    ckv = (_rms(proj(_C_CKV, MLA_KV_RANK)) * kvg_ref[...]).astype(bf16)

    ret_scale = RET_HEAD_DIM ** -0.5
    rq = proj(_C_RQ, RET_WIDTH)
    rk = proj(_C_RK, RET_WIDTH)
    for hd in range(RET_HEADS):
        sl = slice(hd * LANES, (hd + 1) * LANES)
        rq_o[:, sl] = (_rope(rq[:, sl], cr, sr) * ret_scale).astype(bf16)
        rk_o[:, sl] = _rope(rk[:, sl], cr, sr).astype(bf16)
    rv_o[...] = proj(_C_RV, RET_WIDTH).astype(bf16)
    rg_o[...] = _silu(proj(_C_RG, RET_WIDTH))
    mq_o[...] = (proj(_C_MQ, MEM_WIDTH) * (MEM_HEAD_DIM ** -0.5)).astype(bf16)
    kr = _rope(proj(_C_KR, LANES), cm, sm).astype(bf16)

    att_scale = (MLA_NOPE + MLA_ROPE) ** -0.5 * LOG2E
    q = _dot(cq, wuq_ref[...])
    kv = _dot(ckv, wukv_ref[...])
    ones_hi = (lax.broadcasted_iota(jnp.int32, (1, LANES), 1) >= MLA_V).astype(f32)
    for hd in range(MLA_HEADS):
        qn = q[:, hd * MLA_QK:hd * MLA_QK + MLA_NOPE]
        qr = q[:, hd * MLA_QK + MLA_NOPE:(hd + 1) * MLA_QK]
        qm_o[hd, :, :MLA_NOPE] = (qn * att_scale).astype(bf16)
        qm_o[hd, :, MLA_NOPE:] = (_rope(qr, cm, sm) * att_scale).astype(bf16)
        km_o[hd, :, :MLA_NOPE] = kv[:, hd * MLA_NOPE:(hd + 1) * MLA_NOPE].astype(bf16)
        km_o[hd, :, MLA_NOPE:] = kr
        v_lo = MLA_HEADS * MLA_NOPE + hd * MLA_VW
        vm_o[hd] = (kv[:, v_lo:v_lo + MLA_VW] + ones_hi).astype(bf16)


def _inproj(l, x, g, w_in, cos, sin, qg, wuq, kvg, wukv):
    tm = TM_PROJ
    row = lambda width: pl.BlockSpec((tm, width), lambda i: (i, 0))
    headed = lambda width: pl.BlockSpec((MLA_HEADS, tm, width), lambda i: (0, i, 0))
    sds = jax.ShapeDtypeStruct
    return pl.pallas_call(
        _inproj_kernel,
        grid=(SEQ // tm,),
        in_specs=[row(D_MODEL), _layer(l, (1, D_MODEL)), _layer(l, (D_MODEL, IN_COLS)),
                  pl.BlockSpec((2, tm, LANES), lambda i: (0, i, 0)),
                  pl.BlockSpec((2, tm, LANES), lambda i: (0, i, 0)),
                  _layer(l, (1, MLA_Q_RANK)), _layer(l, (MLA_Q_RANK, MLA_HEADS * MLA_QK)),
                  _layer(l, (1, MLA_KV_RANK)), _layer(l, (MLA_KV_RANK, MLA_HEADS * (MLA_NOPE + MLA_VW)))],
        out_specs=[row(RET_WIDTH), row(RET_WIDTH), row(RET_WIDTH), row(RET_WIDTH),
                   headed(MLA_QK), headed(MLA_QK), headed(MLA_VW), row(MEM_WIDTH)],
        out_shape=[sds((SEQ, RET_WIDTH), bf16), sds((SEQ, RET_WIDTH), bf16), sds((SEQ, RET_WIDTH), bf16),
                   sds((SEQ, RET_WIDTH), f32),
                   sds((MLA_HEADS, SEQ, MLA_QK), bf16), sds((MLA_HEADS, SEQ, MLA_QK), bf16),
                   sds((MLA_HEADS, SEQ, MLA_VW), bf16), sds((SEQ, MEM_WIDTH), bf16)],
        compiler_params=_params(("parallel",)),
        name="in_proj",
    )(x, g, w_in, cos, sin, qg, wuq, kvg, wukv)


def _log_gamma(hd):
    return math.log(1.0 - 2.0 ** (-5.0 - hd))


def _retention_kernel(q_ref, k_ref, v_ref, g_ref, o_ref, state, decay, zeta, xi):
    C = RET_CHUNK

    @pl.when(pl.program_id(0) == 0)
    def _():
        state[...] = jnp.zeros_like(state)
        r = lax.broadcasted_iota(jnp.int32, (C, C), 0).astype(f32)
        c = lax.broadcasted_iota(jnp.int32, (C, C), 1).astype(f32)
        diff = r - c
        for hd in range(RET_HEADS):
            lg = _log_gamma(hd)
            decay[hd] = jnp.where(diff >= 0, jnp.exp(lg * jnp.maximum(diff, 0.0)), 0.0)
            zeta[hd] = jnp.exp(lg * (C - 1.0 - r))
            xi[hd] = jnp.exp(lg * (r + 1.0))

    n_chunks = TM_RET // C
    heads = range(RET_HEADS)

    def block(ch, hd):
        return slice(ch * C, (ch + 1) * C), slice(hd * RET_HEAD_DIM, (hd + 1) * RET_HEAD_DIM)

    def state_free(ch):
        out = []
        for hd in heads:
            q, k, v = q_ref[block(ch, hd)], k_ref[block(ch, hd)], v_ref[block(ch, hd)]
            scores = (_nt_dot(q, k) * decay[hd]).astype(bf16)
            kz = (k.astype(f32) * zeta[hd]).astype(bf16)
            out.append((scores, _tn_dot(kz, v)))
        return out

    def finish(ch, pre):
        cross = []
        for hd in heads:
            st = state[hd]
            cross.append(_dot(q_ref[block(ch, hd)], st.astype(bf16)) * xi[hd])
            state[hd] = math.exp(_log_gamma(hd) * C) * st + pre[hd][1]
        for hd in heads:
            o = _dot(pre[hd][0], v_ref[block(ch, hd)]) + cross[hd]
            o_ref[block(ch, hd)] = (g_ref[block(ch, hd)] * _rms(o)).astype(bf16)

    pre_next = state_free(0)
    for ch in range(n_chunks):
        pre = pre_next
        if ch + 1 < n_chunks:
            pre_next = state_free(ch + 1)
        finish(ch, pre)


def _retention(rq, rk, rv, rg):
    tm = TM_RET
    row = pl.BlockSpec((tm, RET_WIDTH), lambda i: (i, 0))
    tab = pltpu.VMEM((RET_HEADS, RET_CHUNK, RET_CHUNK), f32)
    return pl.pallas_call(
        _retention_kernel,
        grid=(SEQ // tm,),
        in_specs=[row, row, row, row],
        out_specs=row,
        out_shape=jax.ShapeDtypeStruct((SEQ, RET_WIDTH), bf16),
        scratch_shapes=[pltpu.VMEM((RET_HEADS, RET_HEAD_DIM, RET_HEAD_DIM), f32), tab, tab, tab],
        compiler_params=_params(("arbitrary",)),
        name="retention",
    )(rq, rk, rv, rg)


def _mla_kernel(q_ref, k_ref, v_ref, o_ref, *scratch):
    T, TK, R, NCH = T_ATT, ATT_KEYS, ATT_ROWS, T_ATT // ATT_ROWS
    m_scs, acc_scs, s_scs = scratch[:NCH], scratch[NCH:2 * NCH], scratch[2 * NCH:]
    qi = pl.program_id(1)
    for c in range(NCH):
        m_scs[c][...] = jnp.full_like(m_scs[c], NEG)
        acc_scs[c][...] = jnp.zeros_like(acc_scs[c])

    def update(c, w, v):
        s_sc = s_scs[c % len(s_scs)]
        m_prev = m_scs[c][...]
        m_new = jnp.maximum(m_prev, jnp.max(s_sc[:, :w], axis=-1, keepdims=True))
        p = jnp.exp2(s_sc[:, :w] - m_new).astype(bf16)
        acc_scs[c][...] = acc_scs[c][...] * jnp.exp2(m_prev - m_new) + _dot(p, v)
        m_scs[c][...] = m_new

    def scores(c, off, w, masked):
        s = _nt_dot(q_ref[c * R:(c + 1) * R, :], k_ref[pl.ds(off, w), :])
        if masked:
            row = lax.broadcasted_iota(jnp.int32, (R, w), 0) + c * R
            col = lax.broadcasted_iota(jnp.int32, (R, w), 1)
            s = jnp.where(col <= row, s, NEG)
        s_scs[c % len(s_scs)][:, :w] = s

    def sweep(off, widths, masked):
        for c in range(min(ATT_AHEAD, NCH)):
            scores(c, off, widths[c], masked)
        for c in range(NCH):
            if c + ATT_AHEAD < NCH:
                scores(c + ATT_AHEAD, off, widths[c + ATT_AHEAD], masked)
            update(c, widths[c], v_ref[pl.ds(off, widths[c]), :])

    def body(j, carry):
        sweep(pl.multiple_of(j * TK, TK), [TK] * NCH, False)
        return carry

    lax.fori_loop(0, qi * (T // TK), body, 0)

    sweep(pl.multiple_of(qi * T, T), [(c + 1) * R for c in range(NCH)], True)

    for c in range(NCH):
        acc = acc_scs[c][...]
        o_ref[c * R:(c + 1) * R, :] = (acc[:, :MLA_V] / acc[:, MLA_V:]).astype(bf16)


def _mla_attention(qm, km, vm):
    T, R = T_ATT, ATT_ROWS
    nch = T // R
    return pl.pallas_call(
        _mla_kernel,
        grid=(MLA_HEADS, SEQ // T),
        in_specs=[pl.BlockSpec((None, T, MLA_QK), lambda h, i: (h, i, 0)),
                  pl.BlockSpec((None, SEQ, MLA_QK), lambda h, i: (h, 0, 0)),
                  pl.BlockSpec((None, SEQ, MLA_VW), lambda h, i: (h, 0, 0))],
        out_specs=pl.BlockSpec((None, T, MLA_V), lambda h, i: (h, i, 0)),
        out_shape=jax.ShapeDtypeStruct((MLA_HEADS, SEQ, MLA_V), bf16),
        scratch_shapes=([pltpu.VMEM((R, 1), f32)] * nch + [pltpu.VMEM((R, MLA_VW), f32)] * nch
                        + [pltpu.VMEM((R, T), f32)] * (ATT_AHEAD + 1)),
        compiler_params=_params(("arbitrary", "arbitrary")),
        name="mla_attention",
    )(qm, km, vm)


def _mix_kernel(x_ref, ret_ref, att_ref, mq_ref, mem_ref, memg_ref, wmem_ref, wo_ref, g_ref, o_ref, kbd_sc, vbd_sc):
    @pl.when(pl.program_id(0) == 0)
    def _():
        mn = (_rms(mem_ref[...]) * memg_ref[...]).astype(bf16)
        kv = _dot(mn, wmem_ref[...])
        k_t = kv[:, :MEM_WIDTH].T
        v = kv[:, MEM_WIDTH:]
        dim_of_row = lax.broadcasted_iota(jnp.int32, (MEM_WIDTH, N_MEM), 0) // MEM_HEAD_DIM
        dim_of_col = lax.broadcasted_iota(jnp.int32, (N_MEM, MEM_WIDTH), 1) // MEM_HEAD_DIM
        for hd in range(MEM_HEADS):
            blk = slice(hd * N_MEM, (hd + 1) * N_MEM)
            kbd_sc[:, blk] = jnp.where(dim_of_row == hd, k_t, 0.0).astype(bf16)
            vbd_sc[blk, :] = jnp.where(dim_of_col == hd, v, 0.0).astype(bf16)

    s_all = _dot(mq_ref[...], kbd_sc[...])
    head_of_lane = lax.broadcasted_iota(jnp.int32, (1, MEM_WIDTH), 1) // MEM_HEAD_DIM
    probs = []
    denom = jnp.zeros((s_all.shape[0], MEM_WIDTH), f32)
    for hd in range(MEM_HEADS):
        s = s_all[:, hd * N_MEM:(hd + 1) * N_MEM]
        p = jnp.exp(s - jnp.max(s, axis=-1, keepdims=True))
        denom = jnp.where(head_of_lane == hd, jnp.sum(p, axis=-1, keepdims=True), denom)
        probs.append(p.astype(bf16))
    mo = _dot(jnp.concatenate(probs, axis=-1), vbd_sc[...]) / denom

    att = jnp.concatenate([att_ref[hd] for hd in range(MLA_HEADS)], axis=-1)
    mixed = _dot(ret_ref[...], wo_ref[:RET_WIDTH, :])
    mixed += _dot(att, wo_ref[RET_WIDTH:RET_WIDTH + MLA_WIDTH, :])
    mixed += _dot(mo.astype(bf16), wo_ref[RET_WIDTH + MLA_WIDTH:, :])
    o_ref[...] = x_ref[...] + _rms(mixed) * g_ref[...]


def _mix(l, x, ret, att, mq, mem, memg, wmem, wo, g):
    tm = TM_PROJ
    row = lambda width: pl.BlockSpec((tm, width), lambda i: (i, 0))
    return pl.pallas_call(
        _mix_kernel,
        grid=(SEQ // tm,),
        in_specs=[row(D_MODEL), row(RET_WIDTH), pl.BlockSpec((MLA_HEADS, tm, MLA_V), lambda i: (0, i, 0)),
                  row(MEM_WIDTH), _const((N_MEM, D_MODEL)), _layer(l, (1, D_MODEL)),
                  _layer(l, (D_MODEL, 2 * MEM_WIDTH)), _layer(l, (D_MODEL, D_MODEL)), _layer(l, (1, D_MODEL))],
        out_specs=row(D_MODEL),
        out_shape=jax.ShapeDtypeStruct((SEQ, D_MODEL), f32),
        scratch_shapes=[pltpu.VMEM((MEM_WIDTH, MEM_HEADS * N_MEM), bf16),
                        pltpu.VMEM((MEM_HEADS * N_MEM, MEM_WIDTH), bf16)],
        compiler_params=_params(("arbitrary",)),
        name="mix_out_proj",
    )(x, ret, att, mq, mem, memg, wmem, wo, g)


def _ffn_kernel(x_ref, gin_ref, wg_ref, wu_ref, wd_ref, gout_ref, o_ref):
    x = x_ref[...]
    h = (_rms(x) * gin_ref[...]).astype(bf16)
    f = jnp.zeros((x.shape[0], D_MODEL), f32)
    for c in range(D_FF // FF_CHUNK):
        cols = slice(c * FF_CHUNK, (c + 1) * FF_CHUNK)
        a = _silu(_dot(h, wg_ref[:, cols])) * _dot(h, wu_ref[:, cols])
        f += _dot(a.astype(bf16), wd_ref[cols, :])
    o_ref[...] = x + _rms(f) * gout_ref[...]


def _ffn(l, x, gin, wg, wu, wd, gout):
    tm = TM_PROJ
    row = pl.BlockSpec((tm, D_MODEL), lambda i: (i, 0))
    return pl.pallas_call(
        _ffn_kernel,
        grid=(SEQ // tm,),
        in_specs=[row, _layer(l, (1, D_MODEL)), _layer(l, (D_MODEL, D_FF)), _layer(l, (D_MODEL, D_FF)),
                  _layer(l, (D_FF, D_MODEL)), _layer(l, (1, D_MODEL))],
        out_specs=row,
        out_shape=jax.ShapeDtypeStruct((SEQ, D_MODEL), f32),
        compiler_params=_params(("parallel",)),
        name="ffn",
    )(x, gin, wg, wu, wd, gout)


def _layout_w_in(w):
    half = MLA_ROPE // 2
    kr0 = 4 * RET_WIDTH + MLA_Q_RANK + MLA_KV_RANK
    kr = w[..., kr0:kr0 + MLA_ROPE]
    z = jnp.zeros(w.shape[:-1] + (half,), w.dtype)
    slab = jnp.concatenate([kr[..., :half], z, kr[..., half:], z], axis=-1)
    return jnp.concatenate([w[..., :kr0], slab, w[..., kr0 + MLA_ROPE:]], axis=-1).astype(bf16)


def _layout_w_uq(w):
    half = MLA_ROPE // 2
    lead = w.shape[:-1]
    w = w.reshape(lead + (MLA_HEADS, MLA_NOPE + MLA_ROPE))
    z = jnp.zeros(lead + (MLA_HEADS, half), w.dtype)
    x1 = w[..., MLA_NOPE:MLA_NOPE + half]
    x2 = w[..., MLA_NOPE + half:]
    return jnp.concatenate([w[..., :MLA_NOPE], x1, z, x2, z], axis=-1).reshape(lead + (-1,)).astype(bf16)


def _layout_w_ukv(w):
    lead = w.shape[:-1]
    w = w.reshape(lead + (MLA_HEADS, MLA_NOPE + MLA_V))
    kn = w[..., :MLA_NOPE].reshape(lead + (-1,))
    v = jnp.concatenate([w[..., MLA_NOPE:], jnp.zeros(lead + (MLA_HEADS, MLA_VW - MLA_V), w.dtype)], axis=-1)
    return jnp.concatenate([kn, v.reshape(lead + (-1,))], axis=-1).astype(bf16)


def kernel(x, mem, positions, pre_mix_g, w_in, mla_q_norm_g, w_uq, mla_kv_norm_g, w_ukv, mem_norm_g, w_mem_kv,
           w_out, post_mix_g, pre_ffn_g, w_gate, w_up, w_down, post_ffn_g):
    assert x.shape == (1, SEQ, D_MODEL) and mem.shape == (1, N_MEM, D_MODEL)
    xs = x[0]
    mem2 = mem[0]
    cos, sin = _rope_tables(positions)
    vec = lambda g: g[:, None, :]
    in_args = (vec(pre_mix_g), _layout_w_in(w_in), cos, sin, vec(mla_q_norm_g), _layout_w_uq(w_uq),
               vec(mla_kv_norm_g), _layout_w_ukv(w_ukv))
    mix_args = (mem2, vec(mem_norm_g), w_mem_kv.astype(bf16), w_out.astype(bf16), vec(post_mix_g))
    ffn_args = (vec(pre_ffn_g), w_gate.astype(bf16), w_up.astype(bf16), w_down.astype(bf16), vec(post_ffn_g))
    for l in range(DEPTH):
        rq, rk, rv, rg, qm, km, vm, mq = _inproj(l, xs, *in_args)
        ret = _retention(rq, rk, rv, rg)
        att = _mla_attention(qm, km, vm)
        xs = _mix(l, xs, ret, att, mq, *mix_args)
        xs = _ffn(l, xs, *ffn_args)
    return xs[None]
```

```python
import functools
import math

import jax
import jax.numpy as jnp
import numpy as np
from jax import lax
from jax.experimental import pallas as pl
from jax.experimental.pallas import tpu as pltpu

D_MODEL = 1024
SEQ = 16384
DEPTH = 4
N_MEM = 256
RET_HEADS = 4
RET_HEAD_DIM = 128
RET_WIDTH = RET_HEADS * RET_HEAD_DIM
RET_CHUNK = 128
MLA_HEADS = 4
MLA_NOPE = 128
MLA_ROPE = 64
MLA_V = 64
MLA_Q_RANK = 256
MLA_KV_RANK = 128
MLA_WIDTH = MLA_HEADS * MLA_V
MEM_HEADS = 4
MEM_HEAD_DIM = 64
MEM_WIDTH = MEM_HEADS * MEM_HEAD_DIM
D_FF = 2816
ROPE_BASE = 10000.0
EPS = 1e-6

LANES = 128
ROW_PACK = 16
MLA_QK = MLA_NOPE + LANES
MLA_VW = LANES
MLA_VT_ROWS = 80

_C_RQ, _C_RK, _C_RV, _C_RG = 0, 512, 1024, 1536
_C_CQ = 2048
_C_CKV = _C_CQ + MLA_Q_RANK
_C_KR = _C_CKV + MLA_KV_RANK
_C_MQ = _C_KR + LANES
IN_COLS = _C_MQ + MEM_WIDTH

TM_PROJ = 512
TM_RET = 512
T_ATT = 2048
ATT_KEYS = 2048
ATT_ROWS = 256
ATT_AHEAD = 3
LOG2E = math.log2(math.e)
FF_CHUNK = 256
VMEM_LIMIT = 56 * 1024 * 1024

f32 = jnp.float32
bf16 = jnp.bfloat16
NEG = float(np.finfo(np.float32).min)


def _nt_dot(a, b):
    return lax.dot_general(a, b, (((1,), (1,)), ((), ())), preferred_element_type=f32)


def _tn_dot(a, b):
    return lax.dot_general(a, b, (((0,), (0,)), ((), ())), preferred_element_type=f32)


def _dot(a, b):
    return jnp.dot(a, b, preferred_element_type=f32)


def _rms(x):
    return x * lax.rsqrt(jnp.mean(x * x, axis=-1, keepdims=True) + EPS)


def _silu(x):
    return x * (1.0 / (1.0 + jnp.exp(-x)))


def _const(shape):
    nd = len(shape)
    return pl.BlockSpec(shape, lambda *_: (0,) * nd)


def _layer(l, tail):
    nd = len(tail)
    return pl.BlockSpec((None,) + tuple(tail), lambda *_: (l,) + (0,) * nd)


def _params(sem):
    return pltpu.CompilerParams(dimension_semantics=sem, vmem_limit_bytes=VMEM_LIMIT)


def _rope_table_kernel(pos_ref, inv_ref, csel_ref, ssel_ref, cos_ref, sin_ref):
    pos = pos_ref[...].astype(f32)
    for t in range(2):
        ang = pos * inv_ref[t:t + 1, :]
        cos_ref[t] = jnp.cos(ang) * csel_ref[t:t + 1, :]
        sin_ref[t] = jnp.sin(ang) * ssel_ref[t:t + 1, :]


def _rope_tables(positions):
    half_r = RET_HEAD_DIM // 2
    half_m = MLA_ROPE // 2
    inv_r = ROPE_BASE ** (-jnp.arange(half_r, dtype=f32) / half_r)
    inv_m = ROPE_BASE ** (-jnp.arange(half_m, dtype=f32) / half_m)
    zeros = jnp.zeros((half_m,), f32)
    ones = jnp.ones((half_m,), f32)
    inv = jnp.stack([jnp.concatenate([inv_r, inv_r]), jnp.concatenate([inv_m, zeros, inv_m, zeros])])
    csel = jnp.stack([jnp.ones((LANES,), f32), jnp.concatenate([ones, zeros, ones, zeros])])
    ssel = jnp.stack([jnp.concatenate([-jnp.ones((half_r,), f32), jnp.ones((half_r,), f32)]),
                      jnp.concatenate([-ones, zeros, ones, zeros])])
    tm = 2048
    tab = jax.ShapeDtypeStruct((2, SEQ, LANES), f32)
    return pl.pallas_call(
        _rope_table_kernel,
        grid=(SEQ // tm,),
        in_specs=[pl.BlockSpec((tm, 1), lambda i: (i, 0)), _const((2, LANES)), _const((2, LANES)),
                  _const((2, LANES))],
        out_specs=[pl.BlockSpec((2, tm, LANES), lambda i: (0, i, 0))] * 2,
        out_shape=[tab, tab],
        compiler_params=_params(("parallel",)),
        name="rope_tables",
    )(positions.reshape(SEQ, 1), inv, csel, ssel)


def _rope(t, c, s):
    return t * c + pltpu.roll(t, LANES // 2, 1) * s


def _inproj_kernel(x_ref, g_ref, w_ref, cos_ref, sin_ref, qg_ref, wuq_ref, kvg_ref, wukv_ref,
                   rq_o, rk_o, rv_o, rg_o, qm_o, km_o, vm_o, mq_o):
    h = (_rms(x_ref[...]) * g_ref[...]).astype(bf16)
    cr, sr = cos_ref[0], sin_ref[0]
    cm, sm = cos_ref[1], sin_ref[1]

    def proj(lo, width):
        return _dot(h, w_ref[:, lo:lo + width])

    cq = (_rms(proj(_C_CQ, MLA_Q_RANK)) * qg_ref[...]).astype(bf16)
    ckv = (_rms(proj(_C_CKV, MLA_KV_RANK)) * kvg_ref[...]).astype(bf16)

    ret_scale = RET_HEAD_DIM ** -0.5
    rq = proj(_C_RQ, RET_WIDTH)
    rk = proj(_C_RK, RET_WIDTH)
    for hd in range(RET_HEADS):
        sl = slice(hd * LANES, (hd + 1) * LANES)
        rq_o[:, sl] = (_rope(rq[:, sl], cr, sr) * ret_scale).astype(bf16)
        rk_o[:, sl] = _rope(rk[:, sl], cr, sr).astype(bf16)
    rv_o[...] = proj(_C_RV, RET_WIDTH).astype(bf16)
    rg_o[...] = _silu(proj(_C_RG, RET_WIDTH))
    mq_o[...] = (proj(_C_MQ, MEM_WIDTH) * (MEM_HEAD_DIM ** -0.5)).astype(bf16)
    kr = _rope(proj(_C_KR, LANES), cm, sm).astype(bf16)

    att_scale = (MLA_NOPE + MLA_ROPE) ** -0.5 * LOG2E
    q = _dot(cq, wuq_ref[...])
    kv = _dot(ckv, wukv_ref[...])
    ones_row = (lax.broadcasted_iota(jnp.int32, (MLA_VT_ROWS, 1), 0) == MLA_V).astype(f32)
    for hd in range(MLA_HEADS):
        qn = q[:, hd * MLA_QK:hd * MLA_QK + MLA_NOPE]
        qr = q[:, hd * MLA_QK + MLA_NOPE:(hd + 1) * MLA_QK]
        qm_o[hd, :, :MLA_NOPE] = (qn * att_scale).astype(bf16)
        qm_o[hd, :, MLA_NOPE:] = (_rope(qr, cm, sm) * att_scale).astype(bf16)
        km_o[hd, :, :MLA_NOPE] = kv[:, hd * MLA_NOPE:(hd + 1) * MLA_NOPE].astype(bf16)
        km_o[hd, :, MLA_NOPE:] = kr
        v_lo = MLA_HEADS * MLA_NOPE + hd * MLA_VW
        vm_o[hd] = (kv[:, v_lo:v_lo + MLA_VW].T[:MLA_VT_ROWS, :] + ones_row).astype(bf16)


def _inproj(l, x, g, w_in, cos, sin, qg, wuq, kvg, wukv):
    tm = TM_PROJ
    row = lambda width: pl.BlockSpec((tm, width), lambda i: (i, 0))
    headed = lambda width: pl.BlockSpec((MLA_HEADS, tm, width), lambda i: (0, i, 0))
    sds = jax.ShapeDtypeStruct
    return pl.pallas_call(
        _inproj_kernel,
        grid=(SEQ // tm,),
        in_specs=[row(D_MODEL), _layer(l, (1, D_MODEL)), _layer(l, (D_MODEL, IN_COLS)),
                  pl.BlockSpec((2, tm, LANES), lambda i: (0, i, 0)),
                  pl.BlockSpec((2, tm, LANES), lambda i: (0, i, 0)),
                  _layer(l, (1, MLA_Q_RANK)), _layer(l, (MLA_Q_RANK, MLA_HEADS * MLA_QK)),
                  _layer(l, (1, MLA_KV_RANK)), _layer(l, (MLA_KV_RANK, MLA_HEADS * (MLA_NOPE + MLA_VW)))],
        out_specs=[row(RET_WIDTH), row(RET_WIDTH), row(RET_WIDTH), row(RET_WIDTH),
                   headed(MLA_QK), headed(MLA_QK),
                   pl.BlockSpec((MLA_HEADS, MLA_VT_ROWS, tm), lambda i: (0, 0, i)), row(MEM_WIDTH)],
        out_shape=[sds((SEQ, RET_WIDTH), bf16), sds((SEQ, RET_WIDTH), bf16), sds((SEQ, RET_WIDTH), bf16),
                   sds((SEQ, RET_WIDTH), f32),
                   sds((MLA_HEADS, SEQ, MLA_QK), bf16), sds((MLA_HEADS, SEQ, MLA_QK), bf16),
                   sds((MLA_HEADS, MLA_VT_ROWS, SEQ), bf16), sds((SEQ, MEM_WIDTH), bf16)],
        compiler_params=_params(("parallel",)),
        name="in_proj",
    )(x, g, w_in, cos, sin, qg, wuq, kvg, wukv)


def _log_gamma(hd):
    return math.log(1.0 - 2.0 ** (-5.0 - hd))


def _retention_kernel(q_ref, k_ref, v_ref, g_ref, o_ref, state, decay, zeta, xi):
    C = RET_CHUNK

    @pl.when(pl.program_id(0) == 0)
    def _():
        state[...] = jnp.zeros_like(state)
        r = lax.broadcasted_iota(jnp.int32, (C, C), 0).astype(f32)
        c = lax.broadcasted_iota(jnp.int32, (C, C), 1).astype(f32)
        diff = r - c
        for hd in range(RET_HEADS):
            lg = _log_gamma(hd)
            decay[hd] = jnp.where(diff >= 0, jnp.exp(lg * jnp.maximum(diff, 0.0)), 0.0)
            zeta[hd] = jnp.exp(lg * (C - 1.0 - r))
            xi[hd] = jnp.exp(lg * (r + 1.0))

    n_chunks = TM_RET // C
    heads = range(RET_HEADS)

    def block(ch, hd):
        return slice(ch * C, (ch + 1) * C), slice(hd * RET_HEAD_DIM, (hd + 1) * RET_HEAD_DIM)

    def state_free(ch):
        out = []
        for hd in heads:
            q, k, v = q_ref[block(ch, hd)], k_ref[block(ch, hd)], v_ref[block(ch, hd)]
            scores = (_nt_dot(q, k) * decay[hd]).astype(bf16)
            kz = (k.astype(f32) * zeta[hd]).astype(bf16)
            out.append((scores, _tn_dot(kz, v)))
        return out

    def finish(ch, pre):
        cross = []
        for hd in heads:
            st = state[hd]
            cross.append(_dot(q_ref[block(ch, hd)], st.astype(bf16)) * xi[hd])
            state[hd] = math.exp(_log_gamma(hd) * C) * st + pre[hd][1]
        for hd in heads:
            o = _dot(pre[hd][0], v_ref[block(ch, hd)]) + cross[hd]
            o_ref[block(ch, hd)] = (g_ref[block(ch, hd)] * _rms(o)).astype(bf16)

    pre_next = state_free(0)
    for ch in range(n_chunks):
        pre = pre_next
        if ch + 1 < n_chunks:
            pre_next = state_free(ch + 1)
        finish(ch, pre)


def _retention(rq, rk, rv, rg):
    tm = TM_RET
    row = pl.BlockSpec((tm, RET_WIDTH), lambda i: (i, 0))
    tab = pltpu.VMEM((RET_HEADS, RET_CHUNK, RET_CHUNK), f32)
    return pl.pallas_call(
        _retention_kernel,
        grid=(SEQ // tm,),
        in_specs=[row, row, row, row],
        out_specs=row,
        out_shape=jax.ShapeDtypeStruct((SEQ, RET_WIDTH), bf16),
        scratch_shapes=[pltpu.VMEM((RET_HEADS, RET_HEAD_DIM, RET_HEAD_DIM), f32), tab, tab, tab],
        compiler_params=_params(("arbitrary",)),
        name="retention",
    )(rq, rk, rv, rg)


def _mla_kernel(q_ref, k_ref, vt_ref, o_ref, *scratch):
    T, TK, R, NCH = T_ATT, ATT_KEYS, ATT_ROWS, T_ATT // ATT_ROWS
    m_scs, acc_scs, s_scs = scratch[:NCH], scratch[NCH:2 * NCH], scratch[2 * NCH:]
    qi = pl.program_id(1)
    for c in range(NCH):
        m_scs[c][...] = jnp.full_like(m_scs[c], NEG)
        acc_scs[c][...] = jnp.zeros_like(acc_scs[c])

    def update(c, off, w):
        s_sc = s_scs[c % len(s_scs)]
        m_prev = m_scs[c][...]
        m_new = jnp.maximum(m_prev, jnp.max(s_sc[:w, :], axis=0, keepdims=True))
        p = jnp.exp2(s_sc[:w, :] - m_new).astype(bf16)
        acc_scs[c][...] = acc_scs[c][...] * jnp.exp2(m_prev - m_new) + _dot(vt_ref[:, pl.ds(off, w)], p)
        m_scs[c][...] = m_new

    def scores(c, off, w, masked):
        s = _nt_dot(k_ref[pl.ds(off, w), :], q_ref[c * R:(c + 1) * R, :])
        if masked:
            key = lax.broadcasted_iota(jnp.int32, (R, R), 0)
            qry = lax.broadcasted_iota(jnp.int32, (R, R), 1)
            if w > R:
                s_scs[c % len(s_scs)][:w - R, :] = s[:w - R, :]
            s_scs[c % len(s_scs)][w - R:w, :] = jnp.where(key <= qry, s[w - R:, :], NEG)
        else:
            s_scs[c % len(s_scs)][:w, :] = s

    def sweep(off, widths, masked):
        for c in range(min(ATT_AHEAD, NCH)):
            scores(c, off, widths[c], masked)
        for c in range(NCH):
            if c + ATT_AHEAD < NCH:
                scores(c + ATT_AHEAD, off, widths[c + ATT_AHEAD], masked)
            update(c, off, widths[c])

    def body(j, carry):
        sweep(pl.multiple_of(j * TK, TK), [TK] * NCH, False)
        return carry

    lax.fori_loop(0, qi * (T // TK), body, 0)

    sweep(pl.multiple_of(qi * T, T), [(c + 1) * R for c in range(NCH)], True)

    for c in range(NCH):
        acc = acc_scs[c][...]
        out_t = acc[:MLA_V, :] / acc[MLA_V:MLA_V + 1, :]
        o_ref[c * R:(c + 1) * R, :] = out_t.T.astype(bf16)


def _mla_attention(qm, km, vm):
    T, R = T_ATT, ATT_ROWS
    nch = T // R
    return pl.pallas_call(
        _mla_kernel,
        grid=(MLA_HEADS, SEQ // T),
        in_specs=[pl.BlockSpec((None, T, MLA_QK), lambda h, i: (h, i, 0)),
                  pl.BlockSpec((None, SEQ, MLA_QK), lambda h, i: (h, 0, 0)),
                  pl.BlockSpec((None, MLA_VT_ROWS, SEQ), lambda h, i: (h, 0, 0))],
        out_specs=pl.BlockSpec((None, T, MLA_V), lambda h, i: (h, i, 0)),
        out_shape=jax.ShapeDtypeStruct((MLA_HEADS, SEQ, MLA_V), bf16),
        scratch_shapes=([pltpu.VMEM((1, R), f32)] * nch + [pltpu.VMEM((MLA_VT_ROWS, R), f32)] * nch
                        + [pltpu.VMEM((T, R), f32)] * (ATT_AHEAD + 1)),
        compiler_params=_params(("arbitrary", "arbitrary")),
        name="mla_attention",
    )(qm, km, vm)


def _mix_kernel(x_ref, ret_ref, att_ref, mq_ref, mem_ref, memg_ref, wmem_ref, wo_ref, g_ref, o_ref, kbd_sc, vbd_sc):
    @pl.when(pl.program_id(0) == 0)
    def _():
        mn = (_rms(mem_ref[...]) * memg_ref[...]).astype(bf16)
        kv = _dot(mn, wmem_ref[...])
        k_t = kv[:, :MEM_WIDTH].T
        v = kv[:, MEM_WIDTH:]
        dim_of_row = lax.broadcasted_iota(jnp.int32, (MEM_WIDTH, N_MEM), 0) // MEM_HEAD_DIM
        dim_of_col = lax.broadcasted_iota(jnp.int32, (N_MEM, MEM_WIDTH), 1) // MEM_HEAD_DIM
        for hd in range(MEM_HEADS):
            blk = slice(hd * N_MEM, (hd + 1) * N_MEM)
            kbd_sc[:, blk] = jnp.where(dim_of_row == hd, k_t, 0.0).astype(bf16)
            vbd_sc[blk, :] = jnp.where(dim_of_col == hd, v, 0.0).astype(bf16)

    s_all = _dot(mq_ref[...], kbd_sc[...])
    head_of_lane = lax.broadcasted_iota(jnp.int32, (1, MEM_WIDTH), 1) // MEM_HEAD_DIM
    probs = []
    denom = jnp.zeros((s_all.shape[0], MEM_WIDTH), f32)
    for hd in range(MEM_HEADS):
        s = s_all[:, hd * N_MEM:(hd + 1) * N_MEM]
        p = jnp.exp(s - jnp.max(s, axis=-1, keepdims=True))
        denom = jnp.where(head_of_lane == hd, jnp.sum(p, axis=-1, keepdims=True), denom)
        probs.append(p.astype(bf16))
    mo = _dot(jnp.concatenate(probs, axis=-1), vbd_sc[...]) / denom

    att = jnp.concatenate([att_ref[hd] for hd in range(MLA_HEADS)], axis=-1)
    mixed = _dot(ret_ref[...], wo_ref[:RET_WIDTH, :])
    mixed += _dot(att, wo_ref[RET_WIDTH:RET_WIDTH + MLA_WIDTH, :])
    mixed += _dot(mo.astype(bf16), wo_ref[RET_WIDTH + MLA_WIDTH:, :])
    o_ref[...] = x_ref[...] + _rms(mixed) * g_ref[...]


def _mix(l, x, ret, att, mq, mem, memg, wmem, wo, g):
    tm = TM_PROJ
    row = lambda width: pl.BlockSpec((tm, width), lambda i: (i, 0))
    return pl.pallas_call(
        _mix_kernel,
        grid=(SEQ // tm,),
        in_specs=[row(D_MODEL), row(RET_WIDTH), pl.BlockSpec((MLA_HEADS, tm, MLA_V), lambda i: (0, i, 0)),
                  row(MEM_WIDTH), _const((N_MEM, D_MODEL)), _layer(l, (1, D_MODEL)),
                  _layer(l, (D_MODEL, 2 * MEM_WIDTH)), _layer(l, (D_MODEL, D_MODEL)), _layer(l, (1, D_MODEL))],
        out_specs=row(D_MODEL),
        out_shape=jax.ShapeDtypeStruct((SEQ, D_MODEL), f32),
        scratch_shapes=[pltpu.VMEM((MEM_WIDTH, MEM_HEADS * N_MEM), bf16),
                        pltpu.VMEM((MEM_HEADS * N_MEM, MEM_WIDTH), bf16)],
        compiler_params=_params(("arbitrary",)),
        name="mix_out_proj",
    )(x, ret, att, mq, mem, memg, wmem, wo, g)


def _ffn_kernel(x_ref, gin_ref, wg_ref, wu_ref, wd_ref, gout_ref, o_ref):
    x = x_ref[...]
    h = (_rms(x) * gin_ref[...]).astype(bf16)
    f = jnp.zeros((x.shape[0], D_MODEL), f32)
    for c in range(D_FF // FF_CHUNK):
        cols = slice(c * FF_CHUNK, (c + 1) * FF_CHUNK)
        a = _silu(_dot(h, wg_ref[:, cols])) * _dot(h, wu_ref[:, cols])
        f += _dot(a.astype(bf16), wd_ref[cols, :])
    o_ref[...] = x + _rms(f) * gout_ref[...]


def _ffn(l, x, gin, wg, wu, wd, gout):
    tm = TM_PROJ
    row = pl.BlockSpec((tm, D_MODEL), lambda i: (i, 0))
    return pl.pallas_call(
        _ffn_kernel,
        grid=(SEQ // tm,),
        in_specs=[row, _layer(l, (1, D_MODEL)), _layer(l, (D_MODEL, D_FF)), _layer(l, (D_MODEL, D_FF)),
                  _layer(l, (D_FF, D_MODEL)), _layer(l, (1, D_MODEL))],
        out_specs=row,
        out_shape=jax.ShapeDtypeStruct((SEQ, D_MODEL), f32),
        compiler_params=_params(("parallel",)),
        name="ffn",
    )(x, gin, wg, wu, wd, gout)


def _layout_w_in(w):
    half = MLA_ROPE // 2
    kr0 = 4 * RET_WIDTH + MLA_Q_RANK + MLA_KV_RANK
    w = w.astype(bf16)
    kr = w[..., kr0:kr0 + MLA_ROPE]
    z = jnp.zeros(w.shape[:-1] + (half,), w.dtype)
    slab = jnp.concatenate([kr[..., :half], z, kr[..., half:], z], axis=-1)
    return jnp.concatenate([w[..., :kr0], slab, w[..., kr0 + MLA_ROPE:]], axis=-1)


def _layout_w_uq(w):
    half = MLA_ROPE // 2
    lead = w.shape[:-1]
    w = w.reshape(lead + (MLA_HEADS, MLA_NOPE + MLA_ROPE))
    z = jnp.zeros(lead + (MLA_HEADS, half), w.dtype)
    x1 = w[..., MLA_NOPE:MLA_NOPE + half]
    x2 = w[..., MLA_NOPE + half:]
    return jnp.concatenate([w[..., :MLA_NOPE], x1, z, x2, z], axis=-1).reshape(lead + (-1,)).astype(bf16)


def _layout_w_ukv(w):
    lead = w.shape[:-1]
    w = w.reshape(lead + (MLA_HEADS, MLA_NOPE + MLA_V))
    kn = w[..., :MLA_NOPE].reshape(lead + (-1,))
    v = jnp.concatenate([w[..., MLA_NOPE:], jnp.zeros(lead + (MLA_HEADS, MLA_VW - MLA_V), w.dtype)], axis=-1)
    return jnp.concatenate([kn, v.reshape(lead + (-1,))], axis=-1).astype(bf16)


def kernel(x, mem, positions, pre_mix_g, w_in, mla_q_norm_g, w_uq, mla_kv_norm_g, w_ukv, mem_norm_g, w_mem_kv,
           w_out, post_mix_g, pre_ffn_g, w_gate, w_up, w_down, post_ffn_g):
    assert x.shape == (1, SEQ, D_MODEL) and mem.shape == (1, N_MEM, D_MODEL)
    xs = x[0]
    mem2 = mem[0]
    cos, sin = _rope_tables(positions)
    vec = lambda g: g[:, None, :]
    in_args = (vec(pre_mix_g), _layout_w_in(w_in), cos, sin, vec(mla_q_norm_g), _layout_w_uq(w_uq),
               vec(mla_kv_norm_g), _layout_w_ukv(w_ukv))
    mix_args = (mem2, vec(mem_norm_g), w_mem_kv.astype(bf16), w_out.astype(bf16), vec(post_mix_g))
    ffn_args = (vec(pre_ffn_g), w_gate.astype(bf16), w_up.astype(bf16), w_down.astype(bf16), vec(post_ffn_g))
    for l in range(DEPTH):
        rq, rk, rv, rg, qm, km, vm, mq = _inproj(l, xs, *in_args)
        ret = _retention(rq, rk, rv, rg)
        att = _mla_attention(qm, km, vm)
        xs = _mix(l, xs, ret, att, mq, *mix_args)
        xs = _ffn(l, xs, *ffn_args)
    return xs[None]
```

```python
import functools
import math

import jax
import jax.numpy as jnp
import numpy as np
from jax import lax
from jax.experimental import pallas as pl
from jax.experimental.pallas import tpu as pltpu

D_MODEL = 1024
SEQ = 16384
DEPTH = 4
N_MEM = 256
RET_HEADS = 4
RET_HEAD_DIM = 128
RET_WIDTH = RET_HEADS * RET_HEAD_DIM
RET_CHUNK = 128
MLA_HEADS = 4
MLA_NOPE = 128
MLA_ROPE = 64
MLA_V = 64
MLA_Q_RANK = 256
MLA_KV_RANK = 128
MLA_WIDTH = MLA_HEADS * MLA_V
MEM_HEADS = 4
MEM_HEAD_DIM = 64
MEM_WIDTH = MEM_HEADS * MEM_HEAD_DIM
D_FF = 2816
ROPE_BASE = 10000.0
EPS = 1e-6

LANES = 128
ROW_PACK = 16
MLA_QK = MLA_NOPE + LANES
MLA_VW = LANES
MLA_VT_ROWS = 80

_C_RQ, _C_RK, _C_RV, _C_RG = 0, 512, 1024, 1536
_C_CQ = 2048
_C_CKV = _C_CQ + MLA_Q_RANK
_C_KR = _C_CKV + MLA_KV_RANK
_C_MQ = _C_KR + LANES
IN_COLS = _C_MQ + MEM_WIDTH

TM_PROJ = 1024
T_ATT = 2048
ATT_KEYS = 2048
ATT_ROWS = 256
ATT_AHEAD = 3
LOG2E = math.log2(math.e)
FF_CHUNK = 256
VMEM_LIMIT = 56 * 1024 * 1024

f32 = jnp.float32
bf16 = jnp.bfloat16
NEG = float(np.finfo(np.float32).min)


def _nt_dot(a, b):
    return lax.dot_general(a, b, (((1,), (1,)), ((), ())), preferred_element_type=f32)


def _tn_dot(a, b):
    return lax.dot_general(a, b, (((0,), (0,)), ((), ())), preferred_element_type=f32)


def _dot(a, b):
    return jnp.dot(a, b, preferred_element_type=f32)


def _rms(x):
    return x * lax.rsqrt(jnp.mean(x * x, axis=-1, keepdims=True) + EPS)


def _silu(x):
    return x * (1.0 / (1.0 + jnp.exp(-x)))


def _const(shape):
    nd = len(shape)
    return pl.BlockSpec(shape, lambda *_: (0,) * nd)


def _layer(l, tail):
    nd = len(tail)
    return pl.BlockSpec((None,) + tuple(tail), lambda *_: (l,) + (0,) * nd)


def _params(sem):
    return pltpu.CompilerParams(dimension_semantics=sem, vmem_limit_bytes=VMEM_LIMIT)


def _rope_table_kernel(pos_ref, inv_ref, cos_ref, sin_ref):
    half_r, half_m = RET_HEAD_DIM // 2, MLA_ROPE // 2
    ang = pos_ref[...].astype(f32) * inv_ref[...]
    c, s = jnp.cos(ang), jnp.sin(ang)
    c_sw, s_sw = pltpu.roll(c, half_r, 1), pltpu.roll(s, half_r, 1)
    lane = lax.broadcasted_iota(jnp.int32, c.shape, 1)
    first_half = lane < half_r
    x1_lanes = lane < half_m
    x2_lanes = (lane >= half_r) & (lane < half_r + half_m)
    cos_ref[0] = jnp.where(first_half, c, c_sw)
    sin_ref[0] = jnp.where(first_half, -s, s_sw)
    cos_ref[1] = jnp.where(x1_lanes, c_sw, jnp.where(x2_lanes, c, 0.0))
    sin_ref[1] = jnp.where(x1_lanes, -s_sw, jnp.where(x2_lanes, s, 0.0))


def _rope_tables(positions):
    half_r = RET_HEAD_DIM // 2
    half_m = MLA_ROPE // 2
    inv_r = ROPE_BASE ** (-jnp.arange(half_r, dtype=f32) / half_r)
    inv_m = ROPE_BASE ** (-jnp.arange(half_m, dtype=f32) / half_m)
    inv = jnp.concatenate([inv_r, inv_m, jnp.zeros((LANES - half_r - half_m,), f32)])[None, :]
    tm = 2048
    tab = jax.ShapeDtypeStruct((2, SEQ, LANES), f32)
    return pl.pallas_call(
        _rope_table_kernel,
        grid=(SEQ // tm,),
        in_specs=[pl.BlockSpec((tm, 1), lambda i: (i, 0)), _const((1, LANES))],
        out_specs=[pl.BlockSpec((2, tm, LANES), lambda i: (0, i, 0))] * 2,
        out_shape=[tab, tab],
        compiler_params=_params(("parallel",)),
        name="rope_tables",
    )(positions.reshape(SEQ, 1), inv)


def _rope(t, c, s):
    return t * c + pltpu.roll(t, LANES // 2, 1) * s


def _inproj_kernel(x_ref, g_ref, w_ref, cos_ref, sin_ref, qg_ref, wuq_ref, kvg_ref, wukv_ref,
                   ret_o, qm_o, km_o, vm_o, mq_o,
                   rq_o, rk_o, rv_o, rg_o, state, decay, zeta, xi):
    @pl.when(pl.program_id(0) == 0)
    def _():
        _retention_init(state, decay, zeta, xi)

    h = (_rms(x_ref[...]) * g_ref[...]).astype(bf16)
    cr, sr = cos_ref[0], sin_ref[0]
    cm, sm = cos_ref[1], sin_ref[1]

    def proj(lo, width):
        return _dot(h, w_ref[:, lo:lo + width])

    cq = (_rms(proj(_C_CQ, MLA_Q_RANK)) * qg_ref[...]).astype(bf16)
    ckv = (_rms(proj(_C_CKV, MLA_KV_RANK)) * kvg_ref[...]).astype(bf16)

    ret_scale = RET_HEAD_DIM ** -0.5
    rq = proj(_C_RQ, RET_WIDTH)
    rk = proj(_C_RK, RET_WIDTH)
    for hd in range(RET_HEADS):
        sl = slice(hd * LANES, (hd + 1) * LANES)
        rq_o[:, sl] = (_rope(rq[:, sl], cr, sr) * ret_scale).astype(bf16)
        rk_o[:, sl] = _rope(rk[:, sl], cr, sr).astype(bf16)
    rv_o[...] = proj(_C_RV, RET_WIDTH).astype(bf16)
    rg_o[...] = _silu(proj(_C_RG, RET_WIDTH))
    mq_o[...] = (proj(_C_MQ, MEM_WIDTH) * (MEM_HEAD_DIM ** -0.5)).astype(bf16)
    kr = _rope(proj(_C_KR, LANES), cm, sm).astype(bf16)

    att_scale = (MLA_NOPE + MLA_ROPE) ** -0.5 * LOG2E
    q = _dot(cq, wuq_ref[...])
    kv = _dot(ckv, wukv_ref[...])
    ones_row = (lax.broadcasted_iota(jnp.int32, (MLA_VT_ROWS, 1), 0) == MLA_V).astype(f32)
    for hd in range(MLA_HEADS):
        qn = q[:, hd * MLA_QK:hd * MLA_QK + MLA_NOPE]
        qr = q[:, hd * MLA_QK + MLA_NOPE:(hd + 1) * MLA_QK]
        qm_o[hd, :, :MLA_NOPE] = (qn * att_scale).astype(bf16)
        qm_o[hd, :, MLA_NOPE:] = (_rope(qr, cm, sm) * att_scale).astype(bf16)
        km_o[hd, :, :MLA_NOPE] = kv[:, hd * MLA_NOPE:(hd + 1) * MLA_NOPE].astype(bf16)
        km_o[hd, :, MLA_NOPE:] = kr
        v_lo = MLA_HEADS * MLA_NOPE + hd * MLA_VW
        vm_o[hd] = (kv[:, v_lo:v_lo + MLA_VW].T[:MLA_VT_ROWS, :] + ones_row).astype(bf16)

    _retention_rows(rq_o, rk_o, rv_o, rg_o, ret_o, state, decay, zeta, xi)


def _inproj(l, x, g, w_in, cos, sin, qg, wuq, kvg, wukv):
    tm = TM_PROJ
    row = lambda width: pl.BlockSpec((tm, width), lambda i: (i, 0))
    headed = lambda width: pl.BlockSpec((MLA_HEADS, tm, width), lambda i: (0, i, 0))
    sds = jax.ShapeDtypeStruct
    ret_tab = pltpu.VMEM((RET_HEADS, RET_CHUNK, RET_CHUNK), f32)
    return pl.pallas_call(
        _inproj_kernel,
        grid=(SEQ // tm,),
        in_specs=[row(D_MODEL), _layer(l, (1, D_MODEL)), _layer(l, (D_MODEL, IN_COLS)),
                  pl.BlockSpec((2, tm, LANES), lambda i: (0, i, 0)),
                  pl.BlockSpec((2, tm, LANES), lambda i: (0, i, 0)),
                  _layer(l, (1, MLA_Q_RANK)), _layer(l, (MLA_Q_RANK, MLA_HEADS * MLA_QK)),
                  _layer(l, (1, MLA_KV_RANK)), _layer(l, (MLA_KV_RANK, MLA_HEADS * (MLA_NOPE + MLA_VW)))],
        out_specs=[row(RET_WIDTH), headed(MLA_QK), headed(MLA_QK),
                   pl.BlockSpec((MLA_HEADS, MLA_VT_ROWS, tm), lambda i: (0, 0, i)), row(MEM_WIDTH)],
        out_shape=[sds((SEQ, RET_WIDTH), bf16),
                   sds((MLA_HEADS, SEQ, MLA_QK), bf16), sds((MLA_HEADS, SEQ, MLA_QK), bf16),
                   sds((MLA_HEADS, MLA_VT_ROWS, SEQ), bf16), sds((SEQ, MEM_WIDTH), bf16)],
        scratch_shapes=[pltpu.VMEM((tm, RET_WIDTH), bf16), pltpu.VMEM((tm, RET_WIDTH), bf16),
                        pltpu.VMEM((tm, RET_WIDTH), bf16), pltpu.VMEM((tm, RET_WIDTH), f32),
                        pltpu.VMEM((RET_HEADS, RET_HEAD_DIM, RET_HEAD_DIM), f32), ret_tab, ret_tab, ret_tab],
        compiler_params=_params(("arbitrary",)),
        name="in_proj",
    )(x, g, w_in, cos, sin, qg, wuq, kvg, wukv)


def _log_gamma(hd):
    return math.log(1.0 - 2.0 ** (-5.0 - hd))


def _retention_init(state, decay, zeta, xi):
    C = RET_CHUNK
    state[...] = jnp.zeros_like(state)
    r = lax.broadcasted_iota(jnp.int32, (C, C), 0).astype(f32)
    c = lax.broadcasted_iota(jnp.int32, (C, C), 1).astype(f32)
    diff = r - c
    for hd in range(RET_HEADS):
        lg = _log_gamma(hd)
        decay[hd] = jnp.where(diff >= 0, jnp.exp(lg * jnp.maximum(diff, 0.0)), 0.0)
        zeta[hd] = jnp.exp(lg * (C - 1.0 - r))
        xi[hd] = jnp.exp(lg * (r + 1.0))


def _retention_rows(q_ref, k_ref, v_ref, g_ref, o_ref, state, decay, zeta, xi):
    C = RET_CHUNK
    n_chunks = q_ref.shape[0] // C
    heads = range(RET_HEADS)

    def block(ch, hd):
        return slice(ch * C, (ch + 1) * C), slice(hd * RET_HEAD_DIM, (hd + 1) * RET_HEAD_DIM)

    def state_free(ch):
        out = []
        for hd in heads:
            q, k, v = q_ref[block(ch, hd)], k_ref[block(ch, hd)], v_ref[block(ch, hd)]
            scores = (_nt_dot(q, k) * decay[hd]).astype(bf16)
            kz = (k.astype(f32) * zeta[hd]).astype(bf16)
            out.append((scores, _tn_dot(kz, v)))
        return out

    def finish(ch, pre):
        cross = []
        for hd in heads:
            st = state[hd]
            cross.append(_dot(q_ref[block(ch, hd)], st.astype(bf16)) * xi[hd])
            state[hd] = math.exp(_log_gamma(hd) * C) * st + pre[hd][1]
        for hd in heads:
            o = _dot(pre[hd][0], v_ref[block(ch, hd)]) + cross[hd]
            o_ref[block(ch, hd)] = (g_ref[block(ch, hd)] * _rms(o)).astype(bf16)

    pre_next = state_free(0)
    for ch in range(n_chunks):
        pre = pre_next
        if ch + 1 < n_chunks:
            pre_next = state_free(ch + 1)
        finish(ch, pre)


def _mla_kernel(q_ref, k_ref, vt_ref, o_ref, *scratch):
    T, TK, R, NCH = T_ATT, ATT_KEYS, ATT_ROWS, T_ATT // ATT_ROWS
    m_scs, acc_scs, s_scs = scratch[:NCH], scratch[NCH:2 * NCH], scratch[2 * NCH:]
    qi = pl.program_id(1)
    for c in range(NCH):
        m_scs[c][...] = jnp.full_like(m_scs[c], NEG)
        acc_scs[c][...] = jnp.zeros_like(acc_scs[c])

    def update(c, off, w):
        s_sc = s_scs[c % len(s_scs)]
        m_prev = m_scs[c][...]
        m_new = jnp.maximum(m_prev, jnp.max(s_sc[:w, :], axis=0, keepdims=True))
        p = jnp.exp2(s_sc[:w, :] - m_new).astype(bf16)
        acc_scs[c][...] = acc_scs[c][...] * jnp.exp2(m_prev - m_new) + _dot(vt_ref[:, pl.ds(off, w)], p)
        m_scs[c][...] = m_new

    def scores(c, off, w, masked):
        s = _nt_dot(k_ref[pl.ds(off, w), :], q_ref[c * R:(c + 1) * R, :])
        if masked:
            key = lax.broadcasted_iota(jnp.int32, (R, R), 0)
            qry = lax.broadcasted_iota(jnp.int32, (R, R), 1)
            if w > R:
                s_scs[c % len(s_scs)][:w - R, :] = s[:w - R, :]
            s_scs[c % len(s_scs)][w - R:w, :] = jnp.where(key <= qry, s[w - R:, :], NEG)
        else:
            s_scs[c % len(s_scs)][:w, :] = s

    def sweep(off, widths, masked):
        for c in range(min(ATT_AHEAD, NCH)):
            scores(c, off, widths[c], masked)
        for c in range(NCH):
            if c + ATT_AHEAD < NCH:
                scores(c + ATT_AHEAD, off, widths[c + ATT_AHEAD], masked)
            update(c, off, widths[c])

    def body(j, carry):
        sweep(pl.multiple_of(j * TK, TK), [TK] * NCH, False)
        return carry

    lax.fori_loop(0, qi * (T // TK), body, 0)

    sweep(pl.multiple_of(qi * T, T), [(c + 1) * R for c in range(NCH)], True)

    for c in range(NCH):
        acc = acc_scs[c][...]
        out_t = acc[:MLA_V, :] / acc[MLA_V:MLA_V + 1, :]
        o_ref[c * R:(c + 1) * R, :] = out_t.T.astype(bf16)


def _mla_attention(qm, km, vm):
    T, R = T_ATT, ATT_ROWS
    nch = T // R
    return pl.pallas_call(
        _mla_kernel,
        grid=(MLA_HEADS, SEQ // T),
        in_specs=[pl.BlockSpec((None, T, MLA_QK), lambda h, i: (h, i, 0)),
                  pl.BlockSpec((None, SEQ, MLA_QK), lambda h, i: (h, 0, 0)),
                  pl.BlockSpec((None, MLA_VT_ROWS, SEQ), lambda h, i: (h, 0, 0))],
        out_specs=pl.BlockSpec((None, T, MLA_V), lambda h, i: (h, i, 0)),
        out_shape=jax.ShapeDtypeStruct((MLA_HEADS, SEQ, MLA_V), bf16),
        scratch_shapes=([pltpu.VMEM((1, R), f32)] * nch + [pltpu.VMEM((MLA_VT_ROWS, R), f32)] * nch
                        + [pltpu.VMEM((T, R), f32)] * (ATT_AHEAD + 1)),
        compiler_params=_params(("arbitrary", "arbitrary")),
        name="mla_attention",
    )(qm, km, vm)


def _mix_kernel(x_ref, ret_ref, att_ref, mq_ref, mem_ref, memg_ref, wmem_ref, wo_ref, g_ref, o_ref, kbd_sc, vbd_sc):
    @pl.when(pl.program_id(0) == 0)
    def _():
        mn = (_rms(mem_ref[...]) * memg_ref[...]).astype(bf16)
        kv = _dot(mn, wmem_ref[...])
        k_t = kv[:, :MEM_WIDTH].T
        v = kv[:, MEM_WIDTH:]
        dim_of_row = lax.broadcasted_iota(jnp.int32, (MEM_WIDTH, N_MEM), 0) // MEM_HEAD_DIM
        dim_of_col = lax.broadcasted_iota(jnp.int32, (N_MEM, MEM_WIDTH), 1) // MEM_HEAD_DIM
        for hd in range(MEM_HEADS):
            blk = slice(hd * N_MEM, (hd + 1) * N_MEM)
            kbd_sc[:, blk] = jnp.where(dim_of_row == hd, k_t, 0.0).astype(bf16)
            vbd_sc[blk, :] = jnp.where(dim_of_col == hd, v, 0.0).astype(bf16)

    s_all = _dot(mq_ref[...], kbd_sc[...])
    att = jnp.concatenate([att_ref[hd] for hd in range(MLA_HEADS)], axis=-1)
    mixed = _dot(ret_ref[...], wo_ref[:RET_WIDTH, :])
    mixed += _dot(att, wo_ref[RET_WIDTH:RET_WIDTH + MLA_WIDTH, :])
    head_of_lane = lax.broadcasted_iota(jnp.int32, (1, MEM_WIDTH), 1) // MEM_HEAD_DIM
    probs = []
    denom = jnp.zeros((s_all.shape[0], MEM_WIDTH), f32)
    for hd in range(MEM_HEADS):
        s = s_all[:, hd * N_MEM:(hd + 1) * N_MEM]
        p = jnp.exp(s - jnp.max(s, axis=-1, keepdims=True))
        denom = jnp.where(head_of_lane == hd, jnp.sum(p, axis=-1, keepdims=True), denom)
        probs.append(p.astype(bf16))
    mo = _dot(jnp.concatenate(probs, axis=-1), vbd_sc[...]) / denom
    mixed += _dot(mo.astype(bf16), wo_ref[RET_WIDTH + MLA_WIDTH:, :])
    o_ref[...] = x_ref[...] + _rms(mixed) * g_ref[...]


def _mix(l, x, ret, att, mq, mem, memg, wmem, wo, g):
    tm = TM_PROJ
    row = lambda width: pl.BlockSpec((tm, width), lambda i: (i, 0))
    return pl.pallas_call(
        _mix_kernel,
        grid=(SEQ // tm,),
        in_specs=[row(D_MODEL), row(RET_WIDTH), pl.BlockSpec((MLA_HEADS, tm, MLA_V), lambda i: (0, i, 0)),
                  row(MEM_WIDTH), _const((N_MEM, D_MODEL)), _layer(l, (1, D_MODEL)),
                  _layer(l, (D_MODEL, 2 * MEM_WIDTH)), _layer(l, (D_MODEL, D_MODEL)), _layer(l, (1, D_MODEL))],
        out_specs=row(D_MODEL),
        out_shape=jax.ShapeDtypeStruct((SEQ, D_MODEL), f32),
        scratch_shapes=[pltpu.VMEM((MEM_WIDTH, MEM_HEADS * N_MEM), bf16),
                        pltpu.VMEM((MEM_HEADS * N_MEM, MEM_WIDTH), bf16)],
        compiler_params=_params(("arbitrary",)),
        name="mix_out_proj",
    )(x, ret, att, mq, mem, memg, wmem, wo, g)


def _ffn_kernel(x_ref, gin_ref, wg_ref, wu_ref, wd_ref, gout_ref, o_ref):
    x = x_ref[...]
    h = (_rms(x) * gin_ref[...]).astype(bf16)
    f = jnp.zeros((x.shape[0], D_MODEL), f32)
    for c in range(D_FF // FF_CHUNK):
        cols = slice(c * FF_CHUNK, (c + 1) * FF_CHUNK)
        a = _silu(_dot(h, wg_ref[:, cols])) * _dot(h, wu_ref[:, cols])
        f += _dot(a.astype(bf16), wd_ref[cols, :])
    o_ref[...] = x + _rms(f) * gout_ref[...]


def _ffn(l, x, gin, wg, wu, wd, gout):
    tm = TM_PROJ
    row = pl.BlockSpec((tm, D_MODEL), lambda i: (i, 0))
    return pl.pallas_call(
        _ffn_kernel,
        grid=(SEQ // tm,),
        in_specs=[row, _layer(l, (1, D_MODEL)), _layer(l, (D_MODEL, D_FF)), _layer(l, (D_MODEL, D_FF)),
                  _layer(l, (D_FF, D_MODEL)), _layer(l, (1, D_MODEL))],
        out_specs=row,
        out_shape=jax.ShapeDtypeStruct((SEQ, D_MODEL), f32),
        compiler_params=_params(("parallel",)),
        name="ffn",
    )(x, gin, wg, wu, wd, gout)


def _layout_w_in(w):
    half = MLA_ROPE // 2
    kr0 = 4 * RET_WIDTH + MLA_Q_RANK + MLA_KV_RANK
    w = w.astype(bf16)
    kr = w[..., kr0:kr0 + MLA_ROPE]
    z = jnp.zeros(w.shape[:-1] + (half,), w.dtype)
    slab = jnp.concatenate([kr[..., :half], z, kr[..., half:], z], axis=-1)
    return jnp.concatenate([w[..., :kr0], slab, w[..., kr0 + MLA_ROPE:]], axis=-1)


def _layout_w_uq(w):
    half = MLA_ROPE // 2
    lead = w.shape[:-1]
    w = w.reshape(lead + (MLA_HEADS, MLA_NOPE + MLA_ROPE))
    z = jnp.zeros(lead + (MLA_HEADS, half), w.dtype)
    x1 = w[..., MLA_NOPE:MLA_NOPE + half]
    x2 = w[..., MLA_NOPE + half:]
    return jnp.concatenate([w[..., :MLA_NOPE], x1, z, x2, z], axis=-1).reshape(lead + (-1,)).astype(bf16)


def _layout_w_ukv(w):
    lead = w.shape[:-1]
    w = w.reshape(lead + (MLA_HEADS, MLA_NOPE + MLA_V))
    kn = w[..., :MLA_NOPE].reshape(lead + (-1,))
    v = jnp.concatenate([w[..., MLA_NOPE:], jnp.zeros(lead + (MLA_HEADS, MLA_VW - MLA_V), w.dtype)], axis=-1)
    return jnp.concatenate([kn, v.reshape(lead + (-1,))], axis=-1).astype(bf16)


def kernel(x, mem, positions, pre_mix_g, w_in, mla_q_norm_g, w_uq, mla_kv_norm_g, w_ukv, mem_norm_g, w_mem_kv,
           w_out, post_mix_g, pre_ffn_g, w_gate, w_up, w_down, post_ffn_g):
    assert x.shape == (1, SEQ, D_MODEL) and mem.shape == (1, N_MEM, D_MODEL)
    xs = x[0]
    mem2 = mem[0]
    cos, sin = _rope_tables(positions)
    vec = lambda g: g[:, None, :]
    in_args = (vec(pre_mix_g), _layout_w_in(w_in), cos, sin, vec(mla_q_norm_g), _layout_w_uq(w_uq),
               vec(mla_kv_norm_g), _layout_w_ukv(w_ukv))
    mix_args = (mem2, vec(mem_norm_g), w_mem_kv.astype(bf16), w_out.astype(bf16), vec(post_mix_g))
    ffn_args = (vec(pre_ffn_g), w_gate.astype(bf16), w_up.astype(bf16), w_down.astype(bf16), vec(post_ffn_g))
    for l in range(DEPTH):
        ret, qm, km, vm, mq = _inproj(l, xs, *in_args)
        att = _mla_attention(qm, km, vm)
        xs = _mix(l, xs, ret, att, mq, *mix_args)
        xs = _ffn(l, xs, *ffn_args)
    return xs[None]
```

```python
import functools
import math

import jax
import jax.numpy as jnp
import numpy as np
from jax import lax
from jax.experimental import pallas as pl
from jax.experimental.pallas import tpu as pltpu

D_MODEL = 1024
SEQ = 16384
DEPTH = 4
N_MEM = 256
RET_HEADS = 4
RET_HEAD_DIM = 128
RET_WIDTH = RET_HEADS * RET_HEAD_DIM
RET_CHUNK = 128
MLA_HEADS = 4
MLA_NOPE = 128
MLA_ROPE = 64
MLA_V = 64
MLA_Q_RANK = 256
MLA_KV_RANK = 128
MLA_WIDTH = MLA_HEADS * MLA_V
MEM_HEADS = 4
MEM_HEAD_DIM = 64
MEM_WIDTH = MEM_HEADS * MEM_HEAD_DIM
D_FF = 2816
ROPE_BASE = 10000.0
EPS = 1e-6

LANES = 128
ROW_PACK = 16
MLA_QK = MLA_NOPE + LANES
MLA_VW = LANES
MLA_VT_ROWS = 80

_C_RQ, _C_RK, _C_RV, _C_RG = 0, 512, 1024, 1536
_C_CQ = 2048
_C_CKV = _C_CQ + MLA_Q_RANK
_C_KR = _C_CKV + MLA_KV_RANK
_T_KR, _T_MQ = 0, LANES
IN_TAIL_COLS = _T_MQ + MEM_WIDTH

TM_PROJ = 1024
T_ATT = 4096
ATT_KEYS = 2048
ATT_ROWS = 256
ATT_AHEAD = 3
LOG2E = math.log2(math.e)
FF_CHUNK = 256
V7X_VMEM_BYTES = 64 * 1024 * 1024
VMEM_LIMIT = V7X_VMEM_BYTES - 4 * 1024 * 1024

f32 = jnp.float32
bf16 = jnp.bfloat16
NEG = float(np.finfo(np.float32).min)


def _nt_dot(a, b):
    return lax.dot_general(a, b, (((1,), (1,)), ((), ())), preferred_element_type=f32)


def _tn_dot(a, b):
    return lax.dot_general(a, b, (((0,), (0,)), ((), ())), preferred_element_type=f32)


def _dot(a, b):
    return jnp.dot(a, b, preferred_element_type=f32)


def _rms(x):
    return x * lax.rsqrt(jnp.mean(x * x, axis=-1, keepdims=True) + EPS)


def _silu(x):
    return x * (1.0 / (1.0 + jnp.exp(-x)))


def _const(shape):
    nd = len(shape)
    return pl.BlockSpec(shape, lambda *_: (0,) * nd)


def _layer(l, tail):
    nd = len(tail)
    return pl.BlockSpec((None,) + tuple(tail), lambda *_: (l,) + (0,) * nd)


def _params(sem):
    return pltpu.CompilerParams(dimension_semantics=sem, vmem_limit_bytes=VMEM_LIMIT)


def _rope_table_kernel(pos_ref, inv_ref, cos_ref, sin_ref):
    half_r, half_m = RET_HEAD_DIM // 2, MLA_ROPE // 2
    ang = pos_ref[...].astype(f32) * inv_ref[...]
    c, s = jnp.cos(ang), jnp.sin(ang)
    c_sw, s_sw = pltpu.roll(c, half_r, 1), pltpu.roll(s, half_r, 1)
    lane = lax.broadcasted_iota(jnp.int32, c.shape, 1)
    first_half = lane < half_r
    x1_lanes = lane < half_m
    x2_lanes = (lane >= half_r) & (lane < half_r + half_m)
    cos_ref[0] = jnp.where(first_half, c, c_sw)
    sin_ref[0] = jnp.where(first_half, -s, s_sw)
    cos_ref[1] = jnp.where(x1_lanes, c_sw, jnp.where(x2_lanes, c, 0.0))
    sin_ref[1] = jnp.where(x1_lanes, -s_sw, jnp.where(x2_lanes, s, 0.0))


def _rope_tables(positions):
    half_r = RET_HEAD_DIM // 2
    half_m = MLA_ROPE // 2
    inv_r = ROPE_BASE ** (-jnp.arange(half_r, dtype=f32) / half_r)
    inv_m = ROPE_BASE ** (-jnp.arange(half_m, dtype=f32) / half_m)
    inv = jnp.concatenate([inv_r, inv_m, jnp.zeros((LANES - half_r - half_m,), f32)])[None, :]
    tm = 2048
    tab = jax.ShapeDtypeStruct((2, SEQ, LANES), f32)
    return pl.pallas_call(
        _rope_table_kernel,
        grid=(SEQ // tm,),
        in_specs=[pl.BlockSpec((tm, 1), lambda i: (i, 0)), _const((1, LANES))],
        out_specs=[pl.BlockSpec((2, tm, LANES), lambda i: (0, i, 0))] * 2,
        out_shape=[tab, tab],
        compiler_params=_params(("parallel",)),
        name="rope_tables",
    )(positions.reshape(SEQ, 1), inv)


def _rope(t, c, s):
    return t * c + pltpu.roll(t, LANES // 2, 1) * s


def _inproj_kernel(x_ref, g_ref, w_ref, wt_ref, cos_ref, sin_ref, qg_ref, wuq_ref, kvg_ref, wukv_ref,
                   ret_o, qm_o, km_o, vm_o, mq_o,
                   rq_o, rk_o, rv_o, rg_o, state, decay, zeta, xi):
    @pl.when(pl.program_id(0) == 0)
    def _():
        _retention_init(state, decay, zeta, xi)

    h = (_rms(x_ref[...]) * g_ref[...]).astype(bf16)
    cr, sr = cos_ref[0], sin_ref[0]
    cm, sm = cos_ref[1], sin_ref[1]

    def proj(lo, width):
        return _dot(h, w_ref[:, lo:lo + width])

    cq = (_rms(proj(_C_CQ, MLA_Q_RANK)) * qg_ref[...]).astype(bf16)
    ckv = (_rms(proj(_C_CKV, MLA_KV_RANK)) * kvg_ref[...]).astype(bf16)

    ret_scale = RET_HEAD_DIM ** -0.5
    rq = proj(_C_RQ, RET_WIDTH)
    rk = proj(_C_RK, RET_WIDTH)
    for hd in range(RET_HEADS):
        sl = slice(hd * LANES, (hd + 1) * LANES)
        rq_o[:, sl] = (_rope(rq[:, sl], cr, sr) * ret_scale).astype(bf16)
        rk_o[:, sl] = _rope(rk[:, sl], cr, sr).astype(bf16)
    rv_o[...] = proj(_C_RV, RET_WIDTH).astype(bf16)
    rg_o[...] = _silu(proj(_C_RG, RET_WIDTH))
    mq_o[...] = (_dot(h, wt_ref[:, _T_MQ:_T_MQ + MEM_WIDTH]) * (MEM_HEAD_DIM ** -0.5)).astype(bf16)
    kr = _rope(_dot(h, wt_ref[:, _T_KR:_T_KR + LANES]), cm, sm).astype(bf16)

    att_scale = (MLA_NOPE + MLA_ROPE) ** -0.5 * LOG2E
    q = _dot(cq, wuq_ref[...])
    kv = _dot(ckv, wukv_ref[...])
    ones_row = (lax.broadcasted_iota(jnp.int32, (MLA_VT_ROWS, 1), 0) == MLA_V).astype(f32)
    for hd in range(MLA_HEADS):
        qn = q[:, hd * MLA_QK:hd * MLA_QK + MLA_NOPE]
        qr = q[:, hd * MLA_QK + MLA_NOPE:(hd + 1) * MLA_QK]
        qm_o[hd, :, :MLA_NOPE] = (qn * att_scale).astype(bf16)
        qm_o[hd, :, MLA_NOPE:] = (_rope(qr, cm, sm) * att_scale).astype(bf16)
        km_o[hd, :, :MLA_NOPE] = kv[:, hd * MLA_NOPE:(hd + 1) * MLA_NOPE].astype(bf16)
        km_o[hd, :, MLA_NOPE:] = kr
        v_lo = MLA_HEADS * MLA_NOPE + hd * MLA_VW
        vm_o[hd] = (kv[:, v_lo:v_lo + MLA_VW].T[:MLA_VT_ROWS, :] + ones_row).astype(bf16)

    _retention_rows(rq_o, rk_o, rv_o, rg_o, ret_o, state, decay, zeta, xi)


def _inproj(l, x, g, w_in, w_tail, cos, sin, qg, wuq, kvg, wukv):
    tm = TM_PROJ
    row = lambda width: pl.BlockSpec((tm, width), lambda i: (i, 0))
    headed = lambda width: pl.BlockSpec((MLA_HEADS, tm, width), lambda i: (0, i, 0))
    sds = jax.ShapeDtypeStruct
    ret_tab = pltpu.VMEM((RET_HEADS, RET_CHUNK, RET_CHUNK), f32)
    return pl.pallas_call(
        _inproj_kernel,
        grid=(SEQ // tm,),
        in_specs=[row(D_MODEL), _layer(l, (1, D_MODEL)), _layer(l, (D_MODEL, _C_KR)),
                  _layer(l, (D_MODEL, IN_TAIL_COLS)),
                  pl.BlockSpec((2, tm, LANES), lambda i: (0, i, 0)),
                  pl.BlockSpec((2, tm, LANES), lambda i: (0, i, 0)),
                  _layer(l, (1, MLA_Q_RANK)), _layer(l, (MLA_Q_RANK, MLA_HEADS * MLA_QK)),
                  _layer(l, (1, MLA_KV_RANK)), _layer(l, (MLA_KV_RANK, MLA_HEADS * (MLA_NOPE + MLA_VW)))],
        out_specs=[row(RET_WIDTH), headed(MLA_QK), headed(MLA_QK),
                   pl.BlockSpec((MLA_HEADS, MLA_VT_ROWS, tm), lambda i: (0, 0, i)), row(MEM_WIDTH)],
        out_shape=[sds((SEQ, RET_WIDTH), bf16),
                   sds((MLA_HEADS, SEQ, MLA_QK), bf16), sds((MLA_HEADS, SEQ, MLA_QK), bf16),
                   sds((MLA_HEADS, MLA_VT_ROWS, SEQ), bf16), sds((SEQ, MEM_WIDTH), bf16)],
        scratch_shapes=[pltpu.VMEM((tm, RET_WIDTH), bf16), pltpu.VMEM((tm, RET_WIDTH), bf16),
                        pltpu.VMEM((tm, RET_WIDTH), bf16), pltpu.VMEM((tm, RET_WIDTH), f32),
                        pltpu.VMEM((RET_HEADS, RET_HEAD_DIM, RET_HEAD_DIM), f32), ret_tab, ret_tab, ret_tab],
        compiler_params=_params(("arbitrary",)),
        name="in_proj",
    )(x, g, w_in, w_tail, cos, sin, qg, wuq, kvg, wukv)


def _log_gamma(hd):
    return math.log(1.0 - 2.0 ** (-5.0 - hd))


def _retention_init(state, decay, zeta, xi):
    C = RET_CHUNK
    state[...] = jnp.zeros_like(state)
    r = lax.broadcasted_iota(jnp.int32, (C, C), 0).astype(f32)
    c = lax.broadcasted_iota(jnp.int32, (C, C), 1).astype(f32)
    diff = r - c
    for hd in range(RET_HEADS):
        lg = _log_gamma(hd)
        decay[hd] = jnp.where(diff >= 0, jnp.exp(lg * jnp.maximum(diff, 0.0)), 0.0)
        zeta[hd] = jnp.exp(lg * (C - 1.0 - r))
        xi[hd] = jnp.exp(lg * (r + 1.0))


def _retention_rows(q_ref, k_ref, v_ref, g_ref, o_ref, state, decay, zeta, xi):
    C = RET_CHUNK
    n_chunks = q_ref.shape[0] // C
    heads = range(RET_HEADS)

    def block(ch, hd):
        return slice(ch * C, (ch + 1) * C), slice(hd * RET_HEAD_DIM, (hd + 1) * RET_HEAD_DIM)

    def state_free(ch):
        out = []
        for hd in heads:
            q, k, v = q_ref[block(ch, hd)], k_ref[block(ch, hd)], v_ref[block(ch, hd)]
            scores = (_nt_dot(q, k) * decay[hd]).astype(bf16)
            kz = (k.astype(f32) * zeta[hd]).astype(bf16)
            out.append((scores, _tn_dot(kz, v)))
        return out

    def finish(ch, pre):
        cross = []
        for hd in heads:
            st = state[hd]
            cross.append(_dot(q_ref[block(ch, hd)], st.astype(bf16)) * xi[hd])
            state[hd] = math.exp(_log_gamma(hd) * C) * st + pre[hd][1]
        for hd in heads:
            o = _dot(pre[hd][0], v_ref[block(ch, hd)]) + cross[hd]
            o_ref[block(ch, hd)] = (g_ref[block(ch, hd)] * _rms(o)).astype(bf16)

    pre_next = state_free(0)
    for ch in range(n_chunks):
        pre = pre_next
        if ch + 1 < n_chunks:
            pre_next = state_free(ch + 1)
        finish(ch, pre)


def _mla_kernel(q_ref, k_ref, vt_ref, o_ref, *scratch):
    T, TK, R, NCH = T_ATT, ATT_KEYS, ATT_ROWS, T_ATT // ATT_ROWS
    m_scs, acc_scs, s_scs = scratch[:NCH], scratch[NCH:2 * NCH], scratch[2 * NCH:]
    qi = pl.program_id(1)
    for c in range(NCH):
        m_scs[c][...] = jnp.full_like(m_scs[c], NEG)
        acc_scs[c][...] = jnp.zeros_like(acc_scs[c])

    def update(c, off, w):
        s_sc = s_scs[c % len(s_scs)]
        m_prev = m_scs[c][...]
        m_new = jnp.maximum(m_prev, jnp.max(s_sc[:w, :], axis=0, keepdims=True))
        p = jnp.exp2(s_sc[:w, :] - m_new).astype(bf16)
        acc_scs[c][...] = acc_scs[c][...] * jnp.exp2(m_prev - m_new) + _dot(vt_ref[:, pl.ds(off, w)], p)
        m_scs[c][...] = m_new

    def scores(c, off, w, masked):
        s = _nt_dot(k_ref[pl.ds(off, w), :], q_ref[c * R:(c + 1) * R, :])
        if masked:
            key = lax.broadcasted_iota(jnp.int32, (R, R), 0)
            qry = lax.broadcasted_iota(jnp.int32, (R, R), 1)
            if w > R:
                s_scs[c % len(s_scs)][:w - R, :] = s[:w - R, :]
            s_scs[c % len(s_scs)][w - R:w, :] = jnp.where(key <= qry, s[w - R:, :], NEG)
        else:
            s_scs[c % len(s_scs)][:w, :] = s

    def sweep(off, widths, masked):
        for c in range(min(ATT_AHEAD, NCH)):
            scores(c, off, widths[c], masked)
        for c in range(NCH):
            if c + ATT_AHEAD < NCH:
                scores(c + ATT_AHEAD, off, widths[c + ATT_AHEAD], masked)
            update(c, off, widths[c])

    def body(j, carry):
        sweep(pl.multiple_of(j * TK, TK), [TK] * NCH, False)
        return carry

    lax.fori_loop(0, qi * (T // TK), body, 0)

    sweep(pl.multiple_of(qi * T, T), [(c + 1) * R for c in range(NCH)], True)

    for c in range(NCH):
        acc = acc_scs[c][...]
        out_t = acc[:MLA_V, :] / acc[MLA_V:MLA_V + 1, :]
        o_ref[c * R:(c + 1) * R, :] = out_t.T.astype(bf16)


def _mla_attention(qm, km, vm):
    T, R = T_ATT, ATT_ROWS
    nch = T // R
    return pl.pallas_call(
        _mla_kernel,
        grid=(MLA_HEADS, SEQ // T),
        in_specs=[pl.BlockSpec((None, T, MLA_QK), lambda h, i: (h, i, 0)),
                  pl.BlockSpec((None, SEQ, MLA_QK), lambda h, i: (h, 0, 0)),
                  pl.BlockSpec((None, MLA_VT_ROWS, SEQ), lambda h, i: (h, 0, 0))],
        out_specs=pl.BlockSpec((None, T, MLA_V), lambda h, i: (h, i, 0)),
        out_shape=jax.ShapeDtypeStruct((MLA_HEADS, SEQ, MLA_V), bf16),
        scratch_shapes=([pltpu.VMEM((1, R), f32)] * nch + [pltpu.VMEM((MLA_VT_ROWS, R), f32)] * nch
                        + [pltpu.VMEM((T, R), f32)] * (ATT_AHEAD + 1)),
        compiler_params=_params(("arbitrary", "arbitrary")),
        name="mla_attention",
    )(qm, km, vm)


def _mix_kernel(x_ref, ret_ref, att_ref, mq_ref, mem_ref, memg_ref, wmem_ref, wo_ref, g_ref, o_ref, kbd_sc, vbd_sc):
    @pl.when(pl.program_id(0) == 0)
    def _():
        mn = (_rms(mem_ref[...]) * memg_ref[...]).astype(bf16)
        kv = _dot(mn, wmem_ref[...])
        k_t = kv[:, :MEM_WIDTH].T
        v = kv[:, MEM_WIDTH:]
        dim_of_row = lax.broadcasted_iota(jnp.int32, (MEM_WIDTH, N_MEM), 0) // MEM_HEAD_DIM
        dim_of_col = lax.broadcasted_iota(jnp.int32, (N_MEM, MEM_WIDTH), 1) // MEM_HEAD_DIM
        for hd in range(MEM_HEADS):
            blk = slice(hd * N_MEM, (hd + 1) * N_MEM)
            kbd_sc[:, blk] = jnp.where(dim_of_row == hd, k_t, 0.0).astype(bf16)
            vbd_sc[blk, :] = jnp.where(dim_of_col == hd, v, 0.0).astype(bf16)

    s_all = _dot(mq_ref[...], kbd_sc[...])
    att = jnp.concatenate([att_ref[hd] for hd in range(MLA_HEADS)], axis=-1)
    mixed = _dot(ret_ref[...], wo_ref[:RET_WIDTH, :])
    mixed += _dot(att, wo_ref[RET_WIDTH:RET_WIDTH + MLA_WIDTH, :])
    head_of_lane = lax.broadcasted_iota(jnp.int32, (1, MEM_WIDTH), 1) // MEM_HEAD_DIM
    probs = []
    denom = jnp.zeros((s_all.shape[0], MEM_WIDTH), f32)
    for hd in range(MEM_HEADS):
        s = s_all[:, hd * N_MEM:(hd + 1) * N_MEM]
        p = jnp.exp(s - jnp.max(s, axis=-1, keepdims=True))
        denom = jnp.where(head_of_lane == hd, jnp.sum(p, axis=-1, keepdims=True), denom)
        probs.append(p.astype(bf16))
    mo = _dot(jnp.concatenate(probs, axis=-1), vbd_sc[...]) / denom
    mixed += _dot(mo.astype(bf16), wo_ref[RET_WIDTH + MLA_WIDTH:, :])
    o_ref[...] = x_ref[...] + _rms(mixed) * g_ref[...]


def _mix(l, x, ret, att, mq, mem, memg, wmem, wo, g):
    tm = TM_PROJ
    row = lambda width: pl.BlockSpec((tm, width), lambda i: (i, 0))
    return pl.pallas_call(
        _mix_kernel,
        grid=(SEQ // tm,),
        in_specs=[row(D_MODEL), row(RET_WIDTH), pl.BlockSpec((MLA_HEADS, tm, MLA_V), lambda i: (0, i, 0)),
                  row(MEM_WIDTH), _const((N_MEM, D_MODEL)), _layer(l, (1, D_MODEL)),
                  _layer(l, (D_MODEL, 2 * MEM_WIDTH)), _layer(l, (D_MODEL, D_MODEL)), _layer(l, (1, D_MODEL))],
        out_specs=row(D_MODEL),
        out_shape=jax.ShapeDtypeStruct((SEQ, D_MODEL), f32),
        scratch_shapes=[pltpu.VMEM((MEM_WIDTH, MEM_HEADS * N_MEM), bf16),
                        pltpu.VMEM((MEM_HEADS * N_MEM, MEM_WIDTH), bf16)],
        compiler_params=_params(("arbitrary",)),
        name="mix_out_proj",
    )(x, ret, att, mq, mem, memg, wmem, wo, g)


def _ffn_kernel(x_ref, gin_ref, wg_ref, wu_ref, wd_ref, gout_ref, o_ref):
    x = x_ref[...]
    h = (_rms(x) * gin_ref[...]).astype(bf16)
    f = jnp.zeros((x.shape[0], D_MODEL), f32)
    for c in range(D_FF // FF_CHUNK):
        cols = slice(c * FF_CHUNK, (c + 1) * FF_CHUNK)
        a = _silu(_dot(h, wg_ref[:, cols])) * _dot(h, wu_ref[:, cols])
        f += _dot(a.astype(bf16), wd_ref[cols, :])
    o_ref[...] = x + _rms(f) * gout_ref[...]


def _ffn(l, x, gin, wg, wu, wd, gout):
    tm = TM_PROJ
    row = pl.BlockSpec((tm, D_MODEL), lambda i: (i, 0))
    return pl.pallas_call(
        _ffn_kernel,
        grid=(SEQ // tm,),
        in_specs=[row, _layer(l, (1, D_MODEL)), _layer(l, (D_MODEL, D_FF)), _layer(l, (D_MODEL, D_FF)),
                  _layer(l, (D_FF, D_MODEL)), _layer(l, (1, D_MODEL))],
        out_specs=row,
        out_shape=jax.ShapeDtypeStruct((SEQ, D_MODEL), f32),
        compiler_params=_params(("parallel",)),
        name="ffn",
    )(x, gin, wg, wu, wd, gout)


def _layout_w_in(w):
    half = MLA_ROPE // 2
    kr = w[..., _C_KR:_C_KR + MLA_ROPE].astype(bf16)
    z = jnp.zeros(kr.shape[:-1] + (half,), bf16)
    tail = jnp.concatenate([kr[..., :half], z, kr[..., half:], z, w[..., _C_KR + MLA_ROPE:].astype(bf16)], axis=-1)
    return w[..., :_C_KR].astype(bf16), tail


def _layout_w_uq(w):
    half = MLA_ROPE // 2
    lead = w.shape[:-1]
    w = w.reshape(lead + (MLA_HEADS, MLA_NOPE + MLA_ROPE))
    z = jnp.zeros(lead + (MLA_HEADS, half), w.dtype)
    x1 = w[..., MLA_NOPE:MLA_NOPE + half]
    x2 = w[..., MLA_NOPE + half:]
    return jnp.concatenate([w[..., :MLA_NOPE], x1, z, x2, z], axis=-1).reshape(lead + (-1,)).astype(bf16)


def _layout_w_ukv(w):
    lead = w.shape[:-1]
    w = w.reshape(lead + (MLA_HEADS, MLA_NOPE + MLA_V))
    kn = w[..., :MLA_NOPE].reshape(lead + (-1,))
    v = jnp.concatenate([w[..., MLA_NOPE:], jnp.zeros(lead + (MLA_HEADS, MLA_VW - MLA_V), w.dtype)], axis=-1)
    return jnp.concatenate([kn, v.reshape(lead + (-1,))], axis=-1).astype(bf16)


def kernel(x, mem, positions, pre_mix_g, w_in, mla_q_norm_g, w_uq, mla_kv_norm_g, w_ukv, mem_norm_g, w_mem_kv,
           w_out, post_mix_g, pre_ffn_g, w_gate, w_up, w_down, post_ffn_g):
    assert x.shape == (1, SEQ, D_MODEL) and mem.shape == (1, N_MEM, D_MODEL)
    xs = x[0]
    mem2 = mem[0]
    cos, sin = _rope_tables(positions)
    vec = lambda g: g[:, None, :]
    in_args = (vec(pre_mix_g), *_layout_w_in(w_in), cos, sin, vec(mla_q_norm_g), _layout_w_uq(w_uq),
               vec(mla_kv_norm_g), _layout_w_ukv(w_ukv))
    mix_args = (mem2, vec(mem_norm_g), w_mem_kv.astype(bf16), w_out.astype(bf16), vec(post_mix_g))
    ffn_args = (vec(pre_ffn_g), w_gate.astype(bf16), w_up.astype(bf16), w_down.astype(bf16), vec(post_ffn_g))
    for l in range(DEPTH):
        ret, qm, km, vm, mq = _inproj(l, xs, *in_args)
        att = _mla_attention(qm, km, vm)
        xs = _mix(l, xs, ret, att, mq, *mix_args)
        xs = _ffn(l, xs, *ffn_args)
    return xs[None]
```

```python
import functools
import math

import jax
import jax.numpy as jnp
import numpy as np
from jax import lax
from jax.experimental import pallas as pl
from jax.experimental.pallas import tpu as pltpu

D_MODEL = 1024
SEQ = 16384
DEPTH = 4
N_MEM = 256
RET_HEADS = 4
RET_HEAD_DIM = 128
RET_WIDTH = RET_HEADS * RET_HEAD_DIM
RET_CHUNK = 128
MLA_HEADS = 4
MLA_NOPE = 128
MLA_ROPE = 64
MLA_V = 64
MLA_Q_RANK = 256
MLA_KV_RANK = 128
MLA_WIDTH = MLA_HEADS * MLA_V
MEM_HEADS = 4
MEM_HEAD_DIM = 64
MEM_WIDTH = MEM_HEADS * MEM_HEAD_DIM
D_FF = 2816
ROPE_BASE = 10000.0
EPS = 1e-6

LANES = 128
ROW_PACK = 16
MLA_QK = MLA_NOPE + LANES
MLA_VW = LANES
MLA_VT_ROWS = 80

_C_RQ, _C_RK, _C_RV, _C_RG = 0, 512, 1024, 1536
_C_CQ = 2048
_C_CKV = _C_CQ + MLA_Q_RANK
_C_KR = _C_CKV + MLA_KV_RANK
_T_KR, _T_MQ = 0, LANES
IN_TAIL_COLS = _T_MQ + MEM_WIDTH

TM_PROJ = 1024
T_ATT = 4096
ATT_KEYS = 2048
ATT_ROWS = 256
ATT_AHEAD = 3
LOG2E = math.log2(math.e)
FF_CHUNK = 256
V7X_VMEM_BYTES = 64 * 1024 * 1024
VMEM_LIMIT = V7X_VMEM_BYTES - 4 * 1024 * 1024

f32 = jnp.float32
bf16 = jnp.bfloat16
NEG = float(np.finfo(np.float32).min)


def _nt_dot(a, b):
    return lax.dot_general(a, b, (((1,), (1,)), ((), ())), preferred_element_type=f32)


def _tn_dot(a, b):
    return lax.dot_general(a, b, (((0,), (0,)), ((), ())), preferred_element_type=f32)


def _dot(a, b):
    return jnp.dot(a, b, preferred_element_type=f32)


def _rms(x):
    return x * lax.rsqrt(jnp.mean(x * x, axis=-1, keepdims=True) + EPS)


def _silu(x):
    return x * (1.0 / (1.0 + jnp.exp(-x)))


def _const(shape):
    nd = len(shape)
    return pl.BlockSpec(shape, lambda *_: (0,) * nd)


def _layer(l, tail):
    nd = len(tail)
    return pl.BlockSpec((None,) + tuple(tail), lambda *_: (l,) + (0,) * nd)


def _params(sem):
    return pltpu.CompilerParams(dimension_semantics=sem, vmem_limit_bytes=VMEM_LIMIT)


def _rope_table_kernel(pos_ref, inv_ref, cos_ref, sin_ref):
    half_r, half_m = RET_HEAD_DIM // 2, MLA_ROPE // 2
    ang = pos_ref[...].astype(f32) * inv_ref[...]
    c, s = jnp.cos(ang), jnp.sin(ang)
    c_sw, s_sw = pltpu.roll(c, half_r, 1), pltpu.roll(s, half_r, 1)
    lane = lax.broadcasted_iota(jnp.int32, c.shape, 1)
    first_half = lane < half_r
    x1_lanes = lane < half_m
    x2_lanes = (lane >= half_r) & (lane < half_r + half_m)
    cos_ref[0] = jnp.where(first_half, c, c_sw)
    sin_ref[0] = jnp.where(first_half, -s, s_sw)
    cos_ref[1] = jnp.where(x1_lanes, c_sw, jnp.where(x2_lanes, c, 0.0))
    sin_ref[1] = jnp.where(x1_lanes, -s_sw, jnp.where(x2_lanes, s, 0.0))


def _rope_tables(positions):
    half_r = RET_HEAD_DIM // 2
    half_m = MLA_ROPE // 2
    inv_r = ROPE_BASE ** (-jnp.arange(half_r, dtype=f32) / half_r)
    inv_m = ROPE_BASE ** (-jnp.arange(half_m, dtype=f32) / half_m)
    inv = jnp.concatenate([inv_r, inv_m, jnp.zeros((LANES - half_r - half_m,), f32)])[None, :]
    tm = 2048
    tab = jax.ShapeDtypeStruct((2, SEQ, LANES), f32)
    return pl.pallas_call(
        _rope_table_kernel,
        grid=(SEQ // tm,),
        in_specs=[pl.BlockSpec((tm, 1), lambda i: (i, 0)), _const((1, LANES))],
        out_specs=[pl.BlockSpec((2, tm, LANES), lambda i: (0, i, 0))] * 2,
        out_shape=[tab, tab],
        compiler_params=_params(("parallel",)),
        name="rope_tables",
    )(positions.reshape(SEQ, 1), inv)


def _rope(t, c, s):
    return t * c + pltpu.roll(t, LANES // 2, 1) * s


def _inproj_kernel(x_ref, g_ref, w_ref, wt_ref, cos_ref, sin_ref, qg_ref, wuq_ref, kvg_ref, wukv_ref,
                   ret_o, qm_o, km_o, vm_o, mq_o,
                   rq_o, rk_o, rv_o, rg_o, state, decay, zeta, xi):
    @pl.when(pl.program_id(0) == 0)
    def _():
        _retention_init(state, decay, zeta, xi)

    h = (_rms(x_ref[...]) * g_ref[...]).astype(bf16)
    cr, sr = cos_ref[0], sin_ref[0]
    cm, sm = cos_ref[1], sin_ref[1]

    def proj(lo, width):
        return _dot(h, w_ref[:, lo:lo + width])

    cq = (_rms(proj(_C_CQ, MLA_Q_RANK)) * qg_ref[...]).astype(bf16)
    ckv = (_rms(proj(_C_CKV, MLA_KV_RANK)) * kvg_ref[...]).astype(bf16)

    ret_scale = RET_HEAD_DIM ** -0.5
    rq = proj(_C_RQ, RET_WIDTH)
    rk = proj(_C_RK, RET_WIDTH)
    for hd in range(RET_HEADS):
        sl = slice(hd * LANES, (hd + 1) * LANES)
        rq_o[:, sl] = (_rope(rq[:, sl], cr, sr) * ret_scale).astype(bf16)
        rk_o[:, sl] = _rope(rk[:, sl], cr, sr).astype(bf16)
    rv_o[...] = proj(_C_RV, RET_WIDTH).astype(bf16)
    rg_o[...] = _silu(proj(_C_RG, RET_WIDTH))
    mq_o[...] = (_dot(h, wt_ref[:, _T_MQ:_T_MQ + MEM_WIDTH]) * (MEM_HEAD_DIM ** -0.5)).astype(bf16)
    kr = _rope(_dot(h, wt_ref[:, _T_KR:_T_KR + LANES]), cm, sm).astype(bf16)

    att_scale = (MLA_NOPE + MLA_ROPE) ** -0.5 * LOG2E
    q = _dot(cq, wuq_ref[...])
    kv = _dot(ckv, wukv_ref[...])
    ones_row = (lax.broadcasted_iota(jnp.int32, (MLA_VT_ROWS, 1), 0) == MLA_V).astype(f32)
    for hd in range(MLA_HEADS):
        qn = q[:, hd * MLA_QK:hd * MLA_QK + MLA_NOPE]
        qr = q[:, hd * MLA_QK + MLA_NOPE:(hd + 1) * MLA_QK]
        qm_o[hd, :, :MLA_NOPE] = (qn * att_scale).astype(bf16)
        qm_o[hd, :, MLA_NOPE:] = (_rope(qr, cm, sm) * att_scale).astype(bf16)
        km_o[hd, :, :MLA_NOPE] = kv[:, hd * MLA_NOPE:(hd + 1) * MLA_NOPE].astype(bf16)
        km_o[hd, :, MLA_NOPE:] = kr
        v_lo = MLA_HEADS * MLA_NOPE + hd * MLA_VW
        vm_o[hd] = (kv[:, v_lo:v_lo + MLA_VW].T[:MLA_VT_ROWS, :] + ones_row).astype(bf16)

    _retention_rows(rq_o, rk_o, rv_o, rg_o, ret_o, state, decay, zeta, xi)


def _inproj(l, x, g, w_in, w_tail, cos, sin, qg, wuq, kvg, wukv):
    tm = TM_PROJ
    row = lambda width: pl.BlockSpec((tm, width), lambda i: (i, 0))
    headed = lambda width: pl.BlockSpec((MLA_HEADS, tm, width), lambda i: (0, i, 0))
    sds = jax.ShapeDtypeStruct
    ret_tab = pltpu.VMEM((RET_HEADS, RET_CHUNK, RET_CHUNK), f32)
    return pl.pallas_call(
        _inproj_kernel,
        grid=(SEQ // tm,),
        in_specs=[row(D_MODEL), _layer(l, (1, D_MODEL)), _layer(l, (D_MODEL, _C_KR)),
                  _layer(l, (D_MODEL, IN_TAIL_COLS)),
                  pl.BlockSpec((2, tm, LANES), lambda i: (0, i, 0)),
                  pl.BlockSpec((2, tm, LANES), lambda i: (0, i, 0)),
                  _layer(l, (1, MLA_Q_RANK)), _layer(l, (MLA_Q_RANK, MLA_HEADS * MLA_QK)),
                  _layer(l, (1, MLA_KV_RANK)), _layer(l, (MLA_KV_RANK, MLA_HEADS * (MLA_NOPE + MLA_VW)))],
        out_specs=[row(RET_WIDTH), headed(MLA_QK), headed(MLA_QK),
                   pl.BlockSpec((MLA_HEADS, MLA_VT_ROWS, tm), lambda i: (0, 0, i)), row(MEM_WIDTH)],
        out_shape=[sds((SEQ, RET_WIDTH), bf16),
                   sds((MLA_HEADS, SEQ, MLA_QK), bf16), sds((MLA_HEADS, SEQ, MLA_QK), bf16),
                   sds((MLA_HEADS, MLA_VT_ROWS, SEQ), bf16), sds((SEQ, MEM_WIDTH), bf16)],
        scratch_shapes=[pltpu.VMEM((tm, RET_WIDTH), bf16), pltpu.VMEM((tm, RET_WIDTH), bf16),
                        pltpu.VMEM((tm, RET_WIDTH), bf16), pltpu.VMEM((tm, RET_WIDTH), f32),
                        pltpu.VMEM((RET_HEADS, RET_HEAD_DIM, RET_HEAD_DIM), f32), ret_tab, ret_tab, ret_tab],
        compiler_params=_params(("arbitrary",)),
        name="in_proj",
    )(x, g, w_in, w_tail, cos, sin, qg, wuq, kvg, wukv)


def _log_gamma(hd):
    return math.log(1.0 - 2.0 ** (-5.0 - hd))


def _retention_init(state, decay, zeta, xi):
    C = RET_CHUNK
    state[...] = jnp.zeros_like(state)
    r = lax.broadcasted_iota(jnp.int32, (C, C), 0).astype(f32)
    c = lax.broadcasted_iota(jnp.int32, (C, C), 1).astype(f32)
    diff = r - c
    for hd in range(RET_HEADS):
        lg = _log_gamma(hd)
        decay[hd] = jnp.where(diff >= 0, jnp.exp(lg * jnp.maximum(diff, 0.0)), 0.0)
        zeta[hd] = jnp.exp(lg * (C - 1.0 - r))
        xi[hd] = jnp.exp(lg * (r + 1.0))


def _retention_rows(q_ref, k_ref, v_ref, g_ref, o_ref, state, decay, zeta, xi):
    C = RET_CHUNK
    n_chunks = q_ref.shape[0] // C
    heads = range(RET_HEADS)

    def block(ch, hd):
        return slice(ch * C, (ch + 1) * C), slice(hd * RET_HEAD_DIM, (hd + 1) * RET_HEAD_DIM)

    def state_free(ch):
        out = []
        for hd in heads:
            q, k, v = q_ref[block(ch, hd)], k_ref[block(ch, hd)], v_ref[block(ch, hd)]
            scores = (_nt_dot(q, k) * decay[hd]).astype(bf16)
            kz = (k.astype(f32) * zeta[hd]).astype(bf16)
            out.append((scores, _tn_dot(kz, v)))
        return out

    def finish(ch, pre):
        cross = []
        for hd in heads:
            st = state[hd]
            cross.append(_dot(q_ref[block(ch, hd)], st.astype(bf16)) * xi[hd])
            state[hd] = math.exp(_log_gamma(hd) * C) * st + pre[hd][1]
        for hd in heads:
            o = _dot(pre[hd][0], v_ref[block(ch, hd)]) + cross[hd]
            o_ref[block(ch, hd)] = (g_ref[block(ch, hd)] * _rms(o)).astype(bf16)

    pre_next = state_free(0)
    for ch in range(n_chunks):
        pre = pre_next
        if ch + 1 < n_chunks:
            pre_next = state_free(ch + 1)
        finish(ch, pre)


def _mla_kernel(q_ref, k_ref, vt_ref, o_ref, *scratch):
    T, TK, R, NCH = T_ATT, ATT_KEYS, ATT_ROWS, T_ATT // ATT_ROWS
    m_scs, acc_scs, s_scs = scratch[:NCH], scratch[NCH:2 * NCH], scratch[2 * NCH:]
    qi = pl.program_id(1)
    for c in range(NCH):
        m_scs[c][...] = jnp.full_like(m_scs[c], NEG)
        acc_scs[c][...] = jnp.zeros_like(acc_scs[c])

    def update(c, off, w):
        s_sc = s_scs[c % len(s_scs)]
        m_prev = m_scs[c][...]
        m_new = jnp.maximum(m_prev, jnp.max(s_sc[:w, :], axis=0, keepdims=True))
        p = jnp.exp2(s_sc[:w, :] - m_new).astype(bf16)
        acc_scs[c][...] = acc_scs[c][...] * jnp.exp2(m_prev - m_new) + _dot(vt_ref[:, pl.ds(off, w)], p)
        m_scs[c][...] = m_new

    def scores(c, off, w, masked):
        s = _nt_dot(k_ref[pl.ds(off, w), :], q_ref[c * R:(c + 1) * R, :])
        if masked:
            key = lax.broadcasted_iota(jnp.int32, (R, R), 0)
            qry = lax.broadcasted_iota(jnp.int32, (R, R), 1)
            if w > R:
                s_scs[c % len(s_scs)][:w - R, :] = s[:w - R, :]
            s_scs[c % len(s_scs)][w - R:w, :] = jnp.where(key <= qry, s[w - R:, :], NEG)
        else:
            s_scs[c % len(s_scs)][:w, :] = s

    def sweep(off, widths, masked):
        for c in range(min(ATT_AHEAD, NCH)):
            scores(c, off, widths[c], masked)
        for c in range(NCH):
            if c + ATT_AHEAD < NCH:
                scores(c + ATT_AHEAD, off, widths[c + ATT_AHEAD], masked)
            update(c, off, widths[c])

    def body(j, carry):
        sweep(pl.multiple_of(j * TK, TK), [TK] * NCH, False)
        return carry

    lax.fori_loop(0, qi * (T // TK), body, 0)

    sweep(pl.multiple_of(qi * T, T), [(c + 1) * R for c in range(NCH)], True)

    for c in range(NCH):
        acc = acc_scs[c][...]
        out_t = acc[:MLA_V, :] / acc[MLA_V:MLA_V + 1, :]
        o_ref[c * R:(c + 1) * R, :] = out_t.T.astype(bf16)


def _mla_attention(qm, km, vm):
    T, R = T_ATT, ATT_ROWS
    nch = T // R
    return pl.pallas_call(
        _mla_kernel,
        grid=(MLA_HEADS, SEQ // T),
        in_specs=[pl.BlockSpec((None, T, MLA_QK), lambda h, i: (h, i, 0)),
                  pl.BlockSpec((None, SEQ, MLA_QK), lambda h, i: (h, 0, 0)),
                  pl.BlockSpec((None, MLA_VT_ROWS, SEQ), lambda h, i: (h, 0, 0))],
        out_specs=pl.BlockSpec((None, T, MLA_V), lambda h, i: (h, i, 0)),
        out_shape=jax.ShapeDtypeStruct((MLA_HEADS, SEQ, MLA_V), bf16),
        scratch_shapes=([pltpu.VMEM((1, R), f32)] * nch + [pltpu.VMEM((MLA_VT_ROWS, R), f32)] * nch
                        + [pltpu.VMEM((T, R), f32)] * (ATT_AHEAD + 1)),
        compiler_params=_params(("arbitrary", "arbitrary")),
        name="mla_attention",
    )(qm, km, vm)


def _mix_kernel(x_ref, ret_ref, att_ref, mq_ref, mem_ref, memg_ref, wmem_ref, wo_ref, g_ref, o_ref, kbd_sc, vbd_sc):
    @pl.when(pl.program_id(0) == 0)
    def _():
        mn = (_rms(mem_ref[...]) * memg_ref[...]).astype(bf16)
        kv = _dot(mn, wmem_ref[...])
        k_t = kv[:, :MEM_WIDTH].T
        v = kv[:, MEM_WIDTH:]
        dim_of_row = lax.broadcasted_iota(jnp.int32, (MEM_WIDTH, N_MEM), 0) // MEM_HEAD_DIM
        dim_of_col = lax.broadcasted_iota(jnp.int32, (N_MEM, MEM_WIDTH), 1) // MEM_HEAD_DIM
        for hd in range(MEM_HEADS):
            blk = slice(hd * N_MEM, (hd + 1) * N_MEM)
            kbd_sc[:, blk] = jnp.where(dim_of_row == hd, k_t, 0.0).astype(bf16)
            vbd_sc[blk, :] = jnp.where(dim_of_col == hd, v, 0.0).astype(bf16)

    s_all = _dot(mq_ref[...], kbd_sc[...])
    att = jnp.concatenate([att_ref[hd] for hd in range(MLA_HEADS)], axis=-1)
    mixed = _dot(ret_ref[...], wo_ref[:RET_WIDTH, :])
    mixed += _dot(att, wo_ref[RET_WIDTH:RET_WIDTH + MLA_WIDTH, :])
    head_of_lane = lax.broadcasted_iota(jnp.int32, (1, MEM_WIDTH), 1) // MEM_HEAD_DIM
    probs = []
    denom = jnp.zeros((s_all.shape[0], MEM_WIDTH), f32)
    for hd in range(MEM_HEADS):
        s = s_all[:, hd * N_MEM:(hd + 1) * N_MEM]
        p = jnp.exp(s - jnp.max(s, axis=-1, keepdims=True))
        denom = jnp.where(head_of_lane == hd, jnp.sum(p, axis=-1, keepdims=True), denom)
        probs.append(p.astype(bf16))
    mo = _dot(jnp.concatenate(probs, axis=-1), vbd_sc[...]) / denom
    mixed += _dot(mo.astype(bf16), wo_ref[RET_WIDTH + MLA_WIDTH:, :])
    o_ref[...] = x_ref[...] + _rms(mixed) * g_ref[...]


def _mix(l, x, ret, att, mq, mem, memg, wmem, wo, g):
    tm = TM_PROJ
    row = lambda width: pl.BlockSpec((tm, width), lambda i: (i, 0))
    return pl.pallas_call(
        _mix_kernel,
        grid=(SEQ // tm,),
        in_specs=[row(D_MODEL), row(RET_WIDTH), pl.BlockSpec((MLA_HEADS, tm, MLA_V), lambda i: (0, i, 0)),
                  row(MEM_WIDTH), _const((N_MEM, D_MODEL)), _layer(l, (1, D_MODEL)),
                  _layer(l, (D_MODEL, 2 * MEM_WIDTH)), _layer(l, (D_MODEL, D_MODEL)), _layer(l, (1, D_MODEL))],
        out_specs=row(D_MODEL),
        out_shape=jax.ShapeDtypeStruct((SEQ, D_MODEL), f32),
        scratch_shapes=[pltpu.VMEM((MEM_WIDTH, MEM_HEADS * N_MEM), bf16),
                        pltpu.VMEM((MEM_HEADS * N_MEM, MEM_WIDTH), bf16)],
        compiler_params=_params(("arbitrary",)),
        name="mix_out_proj",
    )(x, ret, att, mq, mem, memg, wmem, wo, g)


def _ffn_kernel(x_ref, gin_ref, wg_ref, wu_ref, wd_ref, gout_ref, o_ref):
    x = x_ref[...]
    h = (_rms(x) * gin_ref[...]).astype(bf16)
    f = jnp.zeros((x.shape[0], D_MODEL), f32)
    for c in range(D_FF // FF_CHUNK):
        cols = slice(c * FF_CHUNK, (c + 1) * FF_CHUNK)
        a = _silu(_dot(h, wg_ref[:, cols])) * _dot(h, wu_ref[:, cols])
        f += _dot(a.astype(bf16), wd_ref[cols, :])
    o_ref[...] = x + _rms(f) * gout_ref[...]


def _ffn(l, x, gin, wg, wu, wd, gout):
    tm = TM_PROJ
    row = pl.BlockSpec((tm, D_MODEL), lambda i: (i, 0))
    return pl.pallas_call(
        _ffn_kernel,
        grid=(SEQ // tm,),
        in_specs=[row, _layer(l, (1, D_MODEL)), _layer(l, (D_MODEL, D_FF)), _layer(l, (D_MODEL, D_FF)),
                  _layer(l, (D_FF, D_MODEL)), _layer(l, (1, D_MODEL))],
        out_specs=row,
        out_shape=jax.ShapeDtypeStruct((SEQ, D_MODEL), f32),
        compiler_params=_params(("parallel",)),
        name="ffn",
    )(x, gin, wg, wu, wd, gout)


def _layout_w_in(w):
    half = MLA_ROPE // 2
    kr = w[..., _C_KR:_C_KR + MLA_ROPE].astype(bf16)
    z = jnp.zeros(kr.shape[:-1] + (half,), bf16)
    tail = jnp.concatenate([kr[..., :half], z, kr[..., half:], z, w[..., _C_KR + MLA_ROPE:].astype(bf16)], axis=-1)
    return w.astype(bf16), tail


def _layout_w_uq(w):
    half = MLA_ROPE // 2
    lead = w.shape[:-1]
    w = w.reshape(lead + (MLA_HEADS, MLA_NOPE + MLA_ROPE))
    z = jnp.zeros(lead + (MLA_HEADS, half), w.dtype)
    x1 = w[..., MLA_NOPE:MLA_NOPE + half]
    x2 = w[..., MLA_NOPE + half:]
    return jnp.concatenate([w[..., :MLA_NOPE], x1, z, x2, z], axis=-1).reshape(lead + (-1,)).astype(bf16)


def _layout_w_ukv(w):
    lead = w.shape[:-1]
    w = w.reshape(lead + (MLA_HEADS, MLA_NOPE + MLA_V))
    kn = w[..., :MLA_NOPE].reshape(lead + (-1,))
    v = jnp.concatenate([w[..., MLA_NOPE:], jnp.zeros(lead + (MLA_HEADS, MLA_VW - MLA_V), w.dtype)], axis=-1)
    return jnp.concatenate([kn, v.reshape(lead + (-1,))], axis=-1).astype(bf16)


def kernel(x, mem, positions, pre_mix_g, w_in, mla_q_norm_g, w_uq, mla_kv_norm_g, w_ukv, mem_norm_g, w_mem_kv,
           w_out, post_mix_g, pre_ffn_g, w_gate, w_up, w_down, post_ffn_g):
    assert x.shape == (1, SEQ, D_MODEL) and mem.shape == (1, N_MEM, D_MODEL)
    xs = x[0]
    mem2 = mem[0]
    cos, sin = _rope_tables(positions)
    vec = lambda g: g[:, None, :]
    in_args = (vec(pre_mix_g), *_layout_w_in(w_in), cos, sin, vec(mla_q_norm_g), _layout_w_uq(w_uq),
               vec(mla_kv_norm_g), _layout_w_ukv(w_ukv))
    mix_args = (mem2, vec(mem_norm_g), w_mem_kv.astype(bf16), w_out.astype(bf16), vec(post_mix_g))
    ffn_args = (vec(pre_ffn_g), w_gate.astype(bf16), w_up.astype(bf16), w_down.astype(bf16), vec(post_ffn_g))
    for l in range(DEPTH):
        ret, qm, km, vm, mq = _inproj(l, xs, *in_args)
        att = _mla_attention(qm, km, vm)
        xs = _mix(l, xs, ret, att, mq, *mix_args)
        xs = _ffn(l, xs, *ffn_args)
    return xs[None]
```

```python
import functools
import math

import jax
import jax.numpy as jnp
import numpy as np
from jax import lax
from jax.experimental import pallas as pl
from jax.experimental.pallas import tpu as pltpu

D_MODEL = 1024
SEQ = 16384
DEPTH = 4
N_MEM = 256
RET_HEADS = 4
RET_HEAD_DIM = 128
RET_WIDTH = RET_HEADS * RET_HEAD_DIM
RET_CHUNK = 128
MLA_HEADS = 4
MLA_NOPE = 128
MLA_ROPE = 64
MLA_V = 64
MLA_Q_RANK = 256
MLA_KV_RANK = 128
MLA_WIDTH = MLA_HEADS * MLA_V
MEM_HEADS = 4
MEM_HEAD_DIM = 64
MEM_WIDTH = MEM_HEADS * MEM_HEAD_DIM
D_FF = 2816
ROPE_BASE = 10000.0
EPS = 1e-6

LANES = 128
ROW_PACK = 16
MLA_QK = MLA_NOPE + LANES
MLA_VW = LANES
MLA_VT_ROWS = 80

_C_RQ, _C_RK, _C_RV, _C_RG = 0, 512, 1024, 1536
_C_CQ = 2048
_C_CKV = _C_CQ + MLA_Q_RANK
_C_KR = _C_CKV + MLA_KV_RANK
_T_KR, _T_MQ = 0, LANES
IN_TAIL_COLS = _T_MQ + MEM_WIDTH

TM_PROJ = 1024
T_ATT = 4096
ATT_KEYS = 2048
ATT_ROWS = 256
ATT_AHEAD = 3
LOG2E = math.log2(math.e)
FF_CHUNK = 256
V7X_VMEM_BYTES = 64 * 1024 * 1024
VMEM_LIMIT = V7X_VMEM_BYTES - 4 * 1024 * 1024

f32 = jnp.float32
bf16 = jnp.bfloat16
NEG = float(np.finfo(np.float32).min)


def _nt_dot(a, b):
    return lax.dot_general(a, b, (((1,), (1,)), ((), ())), preferred_element_type=f32)


def _tn_dot(a, b):
    return lax.dot_general(a, b, (((0,), (0,)), ((), ())), preferred_element_type=f32)


def _dot(a, b):
    return jnp.dot(a, b, preferred_element_type=f32)


def _rms(x):
    return x * lax.rsqrt(jnp.mean(x * x, axis=-1, keepdims=True) + EPS)


def _silu(x):
    return x * (1.0 / (1.0 + jnp.exp(-x)))


def _const(shape):
    nd = len(shape)
    return pl.BlockSpec(shape, lambda *_: (0,) * nd)


def _layer(l, tail):
    nd = len(tail)
    return pl.BlockSpec((None,) + tuple(tail), lambda *_: (l,) + (0,) * nd)


def _params(sem):
    return pltpu.CompilerParams(dimension_semantics=sem, vmem_limit_bytes=VMEM_LIMIT)


def _rope_table_kernel(pos_ref, inv_ref, cos_ref, sin_ref):
    half_r, half_m = RET_HEAD_DIM // 2, MLA_ROPE // 2
    ang = pos_ref[...].astype(f32) * inv_ref[...]
    c, s = jnp.cos(ang), jnp.sin(ang)
    c_sw, s_sw = pltpu.roll(c, half_r, 1), pltpu.roll(s, half_r, 1)
    lane = lax.broadcasted_iota(jnp.int32, c.shape, 1)
    first_half = lane < half_r
    x1_lanes = lane < half_m
    x2_lanes = (lane >= half_r) & (lane < half_r + half_m)
    cos_ref[0] = jnp.where(first_half, c, c_sw)
    sin_ref[0] = jnp.where(first_half, -s, s_sw)
    cos_ref[1] = jnp.where(x1_lanes, c_sw, jnp.where(x2_lanes, c, 0.0))
    sin_ref[1] = jnp.where(x1_lanes, -s_sw, jnp.where(x2_lanes, s, 0.0))


def _rope_tables(positions):
    half_r = RET_HEAD_DIM // 2
    half_m = MLA_ROPE // 2
    inv_r = ROPE_BASE ** (-jnp.arange(half_r, dtype=f32) / half_r)
    inv_m = ROPE_BASE ** (-jnp.arange(half_m, dtype=f32) / half_m)
    inv = jnp.concatenate([inv_r, inv_m, jnp.zeros((LANES - half_r - half_m,), f32)])[None, :]
    tm = 2048
    tab = jax.ShapeDtypeStruct((2, SEQ, LANES), f32)
    return pl.pallas_call(
        _rope_table_kernel,
        grid=(SEQ // tm,),
        in_specs=[pl.BlockSpec((tm, 1), lambda i: (i, 0)), _const((1, LANES))],
        out_specs=[pl.BlockSpec((2, tm, LANES), lambda i: (0, i, 0))] * 2,
        out_shape=[tab, tab],
        compiler_params=_params(("parallel",)),
        name="rope_tables",
    )(positions.reshape(SEQ, 1), inv)


def _rope(t, c, s):
    return t * c + pltpu.roll(t, LANES // 2, 1) * s


def _inproj_kernel(x_ref, g_ref, w_ref, wt_ref, cos_ref, sin_ref, qg_ref, wuq_ref, kvg_ref, wukv_ref,
                   ret_o, qm_o, km_o, vm_o, mq_o,
                   rq_o, rk_o, rv_o, rg_o, state, decay, zeta, xi):
    @pl.when(pl.program_id(0) == 0)
    def _():
        _retention_init(state, decay, zeta, xi)

    h = (_rms(x_ref[...]) * g_ref[...]).astype(bf16)
    cr, sr = cos_ref[0], sin_ref[0]
    cm, sm = cos_ref[1], sin_ref[1]

    def proj(lo, width):
        return _dot(h, w_ref[:, lo:lo + width])

    cq = (_rms(proj(_C_CQ, MLA_Q_RANK)) * qg_ref[...]).astype(bf16)
    ckv = (_rms(proj(_C_CKV, MLA_KV_RANK)) * kvg_ref[...]).astype(bf16)

    ret_scale = RET_HEAD_DIM ** -0.5
    rq = proj(_C_RQ, RET_WIDTH)
    rk = proj(_C_RK, RET_WIDTH)
    for hd in range(RET_HEADS):
        sl = slice(hd * LANES, (hd + 1) * LANES)
        rq_o[:, sl] = (_rope(rq[:, sl], cr, sr) * ret_scale).astype(bf16)
        rk_o[:, sl] = _rope(rk[:, sl], cr, sr).astype(bf16)
    rv_o[...] = proj(_C_RV, RET_WIDTH).astype(bf16)
    rg_o[...] = _silu(proj(_C_RG, RET_WIDTH))
    mq_o[...] = (_dot(h, wt_ref[:, _T_MQ:_T_MQ + MEM_WIDTH]) * (MEM_HEAD_DIM ** -0.5)).astype(bf16)
    kr = _rope(_dot(h, wt_ref[:, _T_KR:_T_KR + LANES]), cm, sm).astype(bf16)

    att_scale = (MLA_NOPE + MLA_ROPE) ** -0.5 * LOG2E
    q = _dot(cq, wuq_ref[...])
    kv = _dot(ckv, wukv_ref[...])
    ones_row = (lax.broadcasted_iota(jnp.int32, (MLA_VT_ROWS, 1), 0) == MLA_V).astype(f32)
    for hd in range(MLA_HEADS):
        qn = q[:, hd * MLA_QK:hd * MLA_QK + MLA_NOPE]
        qr = q[:, hd * MLA_QK + MLA_NOPE:(hd + 1) * MLA_QK]
        qm_o[hd, :, :MLA_NOPE] = (qn * att_scale).astype(bf16)
        qm_o[hd, :, MLA_NOPE:] = (_rope(qr, cm, sm) * att_scale).astype(bf16)
        km_o[hd, :, :MLA_NOPE] = kv[:, hd * MLA_NOPE:(hd + 1) * MLA_NOPE].astype(bf16)
        km_o[hd, :, MLA_NOPE:] = kr
        v_lo = MLA_HEADS * MLA_NOPE + hd * MLA_VW
        vm_o[hd] = (kv[:, v_lo:v_lo + MLA_VW].T[:MLA_VT_ROWS, :] + ones_row).astype(bf16)

    _retention_rows(rq_o, rk_o, rv_o, rg_o, ret_o, state, decay, zeta, xi)


def _inproj(l, x, g, w_in, w_tail, cos, sin, qg, wuq, kvg, wukv):
    tm = TM_PROJ
    row = lambda width: pl.BlockSpec((tm, width), lambda i: (i, 0))
    headed = lambda width: pl.BlockSpec((MLA_HEADS, tm, width), lambda i: (0, i, 0))
    sds = jax.ShapeDtypeStruct
    ret_tab = pltpu.VMEM((RET_HEADS, RET_CHUNK, RET_CHUNK), f32)
    return pl.pallas_call(
        _inproj_kernel,
        grid=(SEQ // tm,),
        in_specs=[row(D_MODEL), _layer(l, (1, D_MODEL)), _layer(l, (D_MODEL, _C_KR)),
                  _layer(l, (D_MODEL, IN_TAIL_COLS)),
                  pl.BlockSpec((2, tm, LANES), lambda i: (0, i, 0)),
                  pl.BlockSpec((2, tm, LANES), lambda i: (0, i, 0)),
                  _layer(l, (1, MLA_Q_RANK)), _layer(l, (MLA_Q_RANK, MLA_HEADS * MLA_QK)),
                  _layer(l, (1, MLA_KV_RANK)), _layer(l, (MLA_KV_RANK, MLA_HEADS * (MLA_NOPE + MLA_VW)))],
        out_specs=[row(RET_WIDTH), headed(MLA_QK), headed(MLA_QK),
                   pl.BlockSpec((MLA_HEADS, MLA_VT_ROWS, tm), lambda i: (0, 0, i)), row(MEM_WIDTH)],
        out_shape=[sds((SEQ, RET_WIDTH), bf16),
                   sds((MLA_HEADS, SEQ, MLA_QK), bf16), sds((MLA_HEADS, SEQ, MLA_QK), bf16),
                   sds((MLA_HEADS, MLA_VT_ROWS, SEQ), bf16), sds((SEQ, MEM_WIDTH), bf16)],
        scratch_shapes=[pltpu.VMEM((tm, RET_WIDTH), bf16), pltpu.VMEM((tm, RET_WIDTH), bf16),
                        pltpu.VMEM((tm, RET_WIDTH), bf16), pltpu.VMEM((tm, RET_WIDTH), f32),
                        pltpu.VMEM((RET_HEADS, RET_HEAD_DIM, RET_HEAD_DIM), f32), ret_tab, ret_tab, ret_tab],
        compiler_params=_params(("arbitrary",)),
        name="in_proj",
    )(x, g, w_in, w_tail, cos, sin, qg, wuq, kvg, wukv)


def _log_gamma(hd):
    return math.log(1.0 - 2.0 ** (-5.0 - hd))


def _retention_init(state, decay, zeta, xi):
    C = RET_CHUNK
    state[...] = jnp.zeros_like(state)
    r = lax.broadcasted_iota(jnp.int32, (C, C), 0).astype(f32)
    c = lax.broadcasted_iota(jnp.int32, (C, C), 1).astype(f32)
    diff = r - c
    for hd in range(RET_HEADS):
        lg = _log_gamma(hd)
        decay[hd] = jnp.where(diff >= 0, jnp.exp(lg * jnp.maximum(diff, 0.0)), 0.0)
        zeta[hd] = jnp.exp(lg * (C - 1.0 - r))
        xi[hd] = jnp.exp(lg * (r + 1.0))


def _retention_rows(q_ref, k_ref, v_ref, g_ref, o_ref, state, decay, zeta, xi):
    C = RET_CHUNK
    n_chunks = q_ref.shape[0] // C
    heads = range(RET_HEADS)

    def block(ch, hd):
        return slice(ch * C, (ch + 1) * C), slice(hd * RET_HEAD_DIM, (hd + 1) * RET_HEAD_DIM)

    def state_free(ch):
        out = []
        for hd in heads:
            q, k, v = q_ref[block(ch, hd)], k_ref[block(ch, hd)], v_ref[block(ch, hd)]
            scores = (_nt_dot(q, k) * decay[hd]).astype(bf16)
            kz = (k.astype(f32) * zeta[hd]).astype(bf16)
            out.append((scores, _tn_dot(kz, v)))
        return out

    def finish(ch, pre):
        cross = []
        for hd in heads:
            st = state[hd]
            cross.append(_dot(q_ref[block(ch, hd)], st.astype(bf16)) * xi[hd])
            state[hd] = math.exp(_log_gamma(hd) * C) * st + pre[hd][1]
        for hd in heads:
            o = _dot(pre[hd][0], v_ref[block(ch, hd)]) + cross[hd]
            o_ref[block(ch, hd)] = (g_ref[block(ch, hd)] * _rms(o)).astype(bf16)

    pre_next = state_free(0)
    for ch in range(n_chunks):
        pre = pre_next
        if ch + 1 < n_chunks:
            pre_next = state_free(ch + 1)
        finish(ch, pre)


def _mla_kernel(q_ref, k_ref, vt_ref, o_ref, *scratch):
    T, TK, R, NCH = T_ATT, ATT_KEYS, ATT_ROWS, T_ATT // ATT_ROWS
    m_scs, acc_scs, s_scs = scratch[:NCH], scratch[NCH:2 * NCH], scratch[2 * NCH:]
    qi = pl.program_id(1)
    for c in range(NCH):
        m_scs[c][...] = jnp.full_like(m_scs[c], NEG)
        acc_scs[c][...] = jnp.zeros_like(acc_scs[c])

    def update(c, off, w):
        s_sc = s_scs[c % len(s_scs)]
        m_prev = m_scs[c][...]
        m_new = jnp.maximum(m_prev, jnp.max(s_sc[:w, :], axis=0, keepdims=True))
        p = jnp.exp2(s_sc[:w, :] - m_new).astype(bf16)
        acc_scs[c][...] = acc_scs[c][...] * jnp.exp2(m_prev - m_new) + _dot(vt_ref[:, pl.ds(off, w)], p)
        m_scs[c][...] = m_new

    def scores(c, off, w, masked):
        s = _nt_dot(k_ref[pl.ds(off, w), :], q_ref[c * R:(c + 1) * R, :])
        if masked:
            key = lax.broadcasted_iota(jnp.int32, (R, R), 0)
            qry = lax.broadcasted_iota(jnp.int32, (R, R), 1)
            if w > R:
                s_scs[c % len(s_scs)][:w - R, :] = s[:w - R, :]
            s_scs[c % len(s_scs)][w - R:w, :] = jnp.where(key <= qry, s[w - R:, :], NEG)
        else:
            s_scs[c % len(s_scs)][:w, :] = s

    def sweep(off, widths, masked):
        for c in range(min(ATT_AHEAD, NCH)):
            scores(c, off, widths[c], masked)
        for c in range(NCH):
            if c + ATT_AHEAD < NCH:
                scores(c + ATT_AHEAD, off, widths[c + ATT_AHEAD], masked)
            update(c, off, widths[c])

    def body(j, carry):
        sweep(pl.multiple_of(j * TK, TK), [TK] * NCH, False)
        return carry

    lax.fori_loop(0, qi * (T // TK), body, 0)

    sweep(pl.multiple_of(qi * T, T), [(c + 1) * R for c in range(NCH)], True)

    for c in range(NCH):
        acc = acc_scs[c][...]
        out_t = acc[:MLA_V, :] / acc[MLA_V:MLA_V + 1, :]
        o_ref[c * R:(c + 1) * R, :] = out_t.T.astype(bf16)


def _mla_attention(qm, km, vm):
    T, R = T_ATT, ATT_ROWS
    nch = T // R
    return pl.pallas_call(
        _mla_kernel,
        grid=(MLA_HEADS, SEQ // T),
        in_specs=[pl.BlockSpec((None, T, MLA_QK), lambda h, i: (h, i, 0)),
                  pl.BlockSpec((None, SEQ, MLA_QK), lambda h, i: (h, 0, 0)),
                  pl.BlockSpec((None, MLA_VT_ROWS, SEQ), lambda h, i: (h, 0, 0))],
        out_specs=pl.BlockSpec((None, T, MLA_V), lambda h, i: (h, i, 0)),
        out_shape=jax.ShapeDtypeStruct((MLA_HEADS, SEQ, MLA_V), bf16),
        scratch_shapes=([pltpu.VMEM((1, R), f32)] * nch + [pltpu.VMEM((MLA_VT_ROWS, R), f32)] * nch
                        + [pltpu.VMEM((T, R), f32)] * (ATT_AHEAD + 1)),
        compiler_params=_params(("arbitrary", "arbitrary")),
        name="mla_attention",
    )(qm, km, vm)


def _mix_kernel(x_ref, ret_ref, att_ref, mq_ref, mem_ref, memg_ref, wmem_ref, wo_ref, g_ref, o_ref, kbd_sc, vbd_sc):
    @pl.when(pl.program_id(0) == 0)
    def _():
        mn = (_rms(mem_ref[...]) * memg_ref[...]).astype(bf16)
        kv = _dot(mn, wmem_ref[...])
        k_t = kv[:, :MEM_WIDTH].T
        v = kv[:, MEM_WIDTH:]
        dim_of_row = lax.broadcasted_iota(jnp.int32, (MEM_WIDTH, N_MEM), 0) // MEM_HEAD_DIM
        dim_of_col = lax.broadcasted_iota(jnp.int32, (N_MEM, MEM_WIDTH), 1) // MEM_HEAD_DIM
        for hd in range(MEM_HEADS):
            blk = slice(hd * N_MEM, (hd + 1) * N_MEM)
            kbd_sc[:, blk] = jnp.where(dim_of_row == hd, k_t, 0.0).astype(bf16)
            vbd_sc[blk, :] = jnp.where(dim_of_col == hd, v, 0.0).astype(bf16)

    s_all = _dot(mq_ref[...], kbd_sc[...])
    att = jnp.concatenate([att_ref[hd] for hd in range(MLA_HEADS)], axis=-1)
    mixed = _dot(ret_ref[...], wo_ref[:RET_WIDTH, :])
    mixed += _dot(att, wo_ref[RET_WIDTH:RET_WIDTH + MLA_WIDTH, :])
    head_of_lane = lax.broadcasted_iota(jnp.int32, (1, MEM_WIDTH), 1) // MEM_HEAD_DIM
    probs = []
    denom = jnp.zeros((s_all.shape[0], MEM_WIDTH), f32)
    for hd in range(MEM_HEADS):
        s = s_all[:, hd * N_MEM:(hd + 1) * N_MEM]
        p = jnp.exp(s - jnp.max(s, axis=-1, keepdims=True))
        denom = jnp.where(head_of_lane == hd, jnp.sum(p, axis=-1, keepdims=True), denom)
        probs.append(p.astype(bf16))
    mo = _dot(jnp.concatenate(probs, axis=-1), vbd_sc[...]) / denom
    mixed += _dot(mo.astype(bf16), wo_ref[RET_WIDTH + MLA_WIDTH:, :])
    o_ref[...] = x_ref[...] + _rms(mixed) * g_ref[...]


def _mix(l, x, ret, att, mq, mem, memg, wmem, wo, g):
    tm = TM_PROJ
    row = lambda width: pl.BlockSpec((tm, width), lambda i: (i, 0))
    return pl.pallas_call(
        _mix_kernel,
        grid=(SEQ // tm,),
        in_specs=[row(D_MODEL), row(RET_WIDTH), pl.BlockSpec((MLA_HEADS, tm, MLA_V), lambda i: (0, i, 0)),
                  row(MEM_WIDTH), _const((N_MEM, D_MODEL)), _layer(l, (1, D_MODEL)),
                  _layer(l, (D_MODEL, 2 * MEM_WIDTH)), _layer(l, (D_MODEL, D_MODEL)), _layer(l, (1, D_MODEL))],
        out_specs=row(D_MODEL),
        out_shape=jax.ShapeDtypeStruct((SEQ, D_MODEL), f32),
        scratch_shapes=[pltpu.VMEM((MEM_WIDTH, MEM_HEADS * N_MEM), bf16),
                        pltpu.VMEM((MEM_HEADS * N_MEM, MEM_WIDTH), bf16)],
        compiler_params=_params(("arbitrary",)),
        name="mix_out_proj",
    )(x, ret, att, mq, mem, memg, wmem, wo, g)


def _ffn_kernel(x_ref, gin_ref, wg_ref, wu_ref, wd_ref, gout_ref, o_ref):
    x = x_ref[...]
    h = (_rms(x) * gin_ref[...]).astype(bf16)
    f = jnp.zeros((x.shape[0], D_MODEL), f32)
    for c in range(D_FF // FF_CHUNK):
        cols = slice(c * FF_CHUNK, (c + 1) * FF_CHUNK)
        a = _silu(_dot(h, wg_ref[:, cols])) * _dot(h, wu_ref[:, cols])
        f += _dot(a.astype(bf16), wd_ref[cols, :])
    o_ref[...] = x + _rms(f) * gout_ref[...]


def _ffn(l, x, gin, wg, wu, wd, gout):
    tm = TM_PROJ
    row = pl.BlockSpec((tm, D_MODEL), lambda i: (i, 0))
    return pl.pallas_call(
        _ffn_kernel,
        grid=(SEQ // tm,),
        in_specs=[row, _layer(l, (1, D_MODEL)), _layer(l, (D_MODEL, D_FF)), _layer(l, (D_MODEL, D_FF)),
                  _layer(l, (D_FF, D_MODEL)), _layer(l, (1, D_MODEL))],
        out_specs=row,
        out_shape=jax.ShapeDtypeStruct((SEQ, D_MODEL), f32),
        compiler_params=_params(("parallel",)),
        name="ffn",
    )(x, gin, wg, wu, wd, gout)


def _layout_w_in(w):
    half = MLA_ROPE // 2
    wb = w.astype(bf16)
    t = wb[..., _C_KR:]
    z = jnp.zeros(t.shape[:-1] + (half,), bf16)
    tail = jnp.concatenate([t[..., :half], z, t[..., half:MLA_ROPE], z, t[..., MLA_ROPE:]], axis=-1)
    return wb, tail


def _layout_w_uq(w):
    half = MLA_ROPE // 2
    lead = w.shape[:-1]
    w = w.reshape(lead + (MLA_HEADS, MLA_NOPE + MLA_ROPE))
    z = jnp.zeros(lead + (MLA_HEADS, half), w.dtype)
    x1 = w[..., MLA_NOPE:MLA_NOPE + half]
    x2 = w[..., MLA_NOPE + half:]
    return jnp.concatenate([w[..., :MLA_NOPE], x1, z, x2, z], axis=-1).reshape(lead + (-1,)).astype(bf16)


def _layout_w_ukv(w):
    lead = w.shape[:-1]
    w = w.reshape(lead + (MLA_HEADS, MLA_NOPE + MLA_V))
    kn = w[..., :MLA_NOPE].reshape(lead + (-1,))
    v = jnp.concatenate([w[..., MLA_NOPE:], jnp.zeros(lead + (MLA_HEADS, MLA_VW - MLA_V), w.dtype)], axis=-1)
    return jnp.concatenate([kn, v.reshape(lead + (-1,))], axis=-1).astype(bf16)


def kernel(x, mem, positions, pre_mix_g, w_in, mla_q_norm_g, w_uq, mla_kv_norm_g, w_ukv, mem_norm_g, w_mem_kv,
           w_out, post_mix_g, pre_ffn_g, w_gate, w_up, w_down, post_ffn_g):
    assert x.shape == (1, SEQ, D_MODEL) and mem.shape == (1, N_MEM, D_MODEL)
    xs = x[0]
    mem2 = mem[0]
    cos, sin = _rope_tables(positions)
    vec = lambda g: g[:, None, :]
    in_args = (vec(pre_mix_g), *_layout_w_in(w_in), cos, sin, vec(mla_q_norm_g), _layout_w_uq(w_uq),
               vec(mla_kv_norm_g), _layout_w_ukv(w_ukv))
    mix_args = (mem2, vec(mem_norm_g), w_mem_kv.astype(bf16), w_out.astype(bf16), vec(post_mix_g))
    ffn_args = (vec(pre_ffn_g), w_gate.astype(bf16), w_up.astype(bf16), w_down.astype(bf16), vec(post_ffn_g))
    for l in range(DEPTH):
        ret, qm, km, vm, mq = _inproj(l, xs, *in_args)
        att = _mla_attention(qm, km, vm)
        xs = _mix(l, xs, ret, att, mq, *mix_args)
        xs = _ffn(l, xs, *ffn_args)
    return xs[None]
```

```python
import functools
import math

import jax
import jax.numpy as jnp
import numpy as np
from jax import lax
from jax.experimental import pallas as pl
from jax.experimental.pallas import tpu as pltpu

D_MODEL = 1024
SEQ = 16384
DEPTH = 4
N_MEM = 256
RET_HEADS = 4
RET_HEAD_DIM = 128
RET_WIDTH = RET_HEADS * RET_HEAD_DIM
RET_CHUNK = 128
MLA_HEADS = 4
MLA_NOPE = 128
MLA_ROPE = 64
MLA_V = 64
MLA_Q_RANK = 256
MLA_KV_RANK = 128
MLA_WIDTH = MLA_HEADS * MLA_V
MEM_HEADS = 4
MEM_HEAD_DIM = 64
MEM_WIDTH = MEM_HEADS * MEM_HEAD_DIM
D_FF = 2816
ROPE_BASE = 10000.0
EPS = 1e-6

LANES = 128
ROW_PACK = 16
MLA_QK = MLA_NOPE + LANES
MLA_VW = LANES
MLA_VT_ROWS = 80

_C_RQ, _C_RK, _C_RV, _C_RG = 0, 512, 1024, 1536
_C_CQ = 2048
_C_CKV = _C_CQ + MLA_Q_RANK
_C_KR = _C_CKV + MLA_KV_RANK
_T_KR, _T_MQ = 0, LANES
IN_TAIL_COLS = _T_MQ + MEM_WIDTH

TM_PROJ = 1024
T_ATT = 4096
ATT_KEYS = 2048
ATT_ROWS = 256
ATT_AHEAD = 3
LOG2E = math.log2(math.e)
FF_CHUNK = 256
V7X_VMEM_BYTES = 64 * 1024 * 1024
VMEM_LIMIT = V7X_VMEM_BYTES - 4 * 1024 * 1024

f32 = jnp.float32
bf16 = jnp.bfloat16
NEG = float(np.finfo(np.float32).min)


def _nt_dot(a, b):
    return lax.dot_general(a, b, (((1,), (1,)), ((), ())), preferred_element_type=f32)


def _tn_dot(a, b):
    return lax.dot_general(a, b, (((0,), (0,)), ((), ())), preferred_element_type=f32)


def _dot(a, b):
    return jnp.dot(a, b, preferred_element_type=f32)


def _rms(x):
    return x * lax.rsqrt(jnp.mean(x * x, axis=-1, keepdims=True) + EPS)


def _silu(x):
    return x * (1.0 / (1.0 + jnp.exp(-x)))


def _const(shape):
    nd = len(shape)
    return pl.BlockSpec(shape, lambda *_: (0,) * nd)


def _layer(l, tail):
    nd = len(tail)
    return pl.BlockSpec((None,) + tuple(tail), lambda *_: (l,) + (0,) * nd)


def _params(sem):
    return pltpu.CompilerParams(dimension_semantics=sem, vmem_limit_bytes=VMEM_LIMIT)


def _rope_table_kernel(pos_ref, inv_ref, cos_ref, sin_ref):
    half_r, half_m = RET_HEAD_DIM // 2, MLA_ROPE // 2
    ang = pos_ref[...].astype(f32) * inv_ref[...]
    c, s = jnp.cos(ang), jnp.sin(ang)
    c_sw, s_sw = pltpu.roll(c, half_r, 1), pltpu.roll(s, half_r, 1)
    lane = lax.broadcasted_iota(jnp.int32, c.shape, 1)
    first_half = lane < half_r
    x1_lanes = lane < half_m
    x2_lanes = (lane >= half_r) & (lane < half_r + half_m)
    cos_ref[0] = jnp.where(first_half, c, c_sw)
    sin_ref[0] = jnp.where(first_half, -s, s_sw)
    cos_ref[1] = jnp.where(x1_lanes, c_sw, jnp.where(x2_lanes, c, 0.0))
    sin_ref[1] = jnp.where(x1_lanes, -s_sw, jnp.where(x2_lanes, s, 0.0))


def _rope_tables(positions):
    half_r = RET_HEAD_DIM // 2
    half_m = MLA_ROPE // 2
    inv_r = ROPE_BASE ** (-jnp.arange(half_r, dtype=f32) / half_r)
    inv_m = ROPE_BASE ** (-jnp.arange(half_m, dtype=f32) / half_m)
    inv = jnp.concatenate([inv_r, inv_m, jnp.zeros((LANES - half_r - half_m,), f32)])[None, :]
    tm = 2048
    tab = jax.ShapeDtypeStruct((2, SEQ, LANES), f32)
    return pl.pallas_call(
        _rope_table_kernel,
        grid=(SEQ // tm,),
        in_specs=[pl.BlockSpec((tm, 1), lambda i: (i, 0)), _const((1, LANES))],
        out_specs=[pl.BlockSpec((2, tm, LANES), lambda i: (0, i, 0))] * 2,
        out_shape=[tab, tab],
        compiler_params=_params(("parallel",)),
        name="rope_tables",
    )(positions.reshape(SEQ, 1), inv)


def _rope(t, c, s):
    return t * c + pltpu.roll(t, LANES // 2, 1) * s


def _inproj_kernel(x_ref, g_ref, w_ref, wt_ref, cos_ref, sin_ref, qg_ref, wuq_ref, kvg_ref, wukv_ref,
                   ret_o, qm_o, km_o, vm_o, mq_o,
                   rq_o, rk_o, rv_o, rg_o, state, decay, zeta, xi):
    @pl.when(pl.program_id(0) == 0)
    def _():
        _retention_init(state, decay, zeta, xi)

    h = (_rms(x_ref[...]) * g_ref[...]).astype(bf16)
    cr, sr = cos_ref[0], sin_ref[0]
    cm, sm = cos_ref[1], sin_ref[1]

    def proj(lo, width):
        return _dot(h, w_ref[:, lo:lo + width])

    cq = (_rms(proj(_C_CQ, MLA_Q_RANK)) * qg_ref[...]).astype(bf16)
    ckv = (_rms(proj(_C_CKV, MLA_KV_RANK)) * kvg_ref[...]).astype(bf16)

    ret_scale = RET_HEAD_DIM ** -0.5
    rq = proj(_C_RQ, RET_WIDTH)
    rk = proj(_C_RK, RET_WIDTH)
    for hd in range(RET_HEADS):
        sl = slice(hd * LANES, (hd + 1) * LANES)
        rq_o[:, sl] = (_rope(rq[:, sl], cr, sr) * ret_scale).astype(bf16)
        rk_o[:, sl] = _rope(rk[:, sl], cr, sr).astype(bf16)
    rv_o[...] = proj(_C_RV, RET_WIDTH).astype(bf16)
    rg_o[...] = _silu(proj(_C_RG, RET_WIDTH))
    mq_o[...] = (_dot(h, wt_ref[:, _T_MQ:_T_MQ + MEM_WIDTH]) * (MEM_HEAD_DIM ** -0.5)).astype(bf16)
    kr = _rope(_dot(h, wt_ref[:, _T_KR:_T_KR + LANES]), cm, sm).astype(bf16)

    att_scale = (MLA_NOPE + MLA_ROPE) ** -0.5 * LOG2E
    q = _dot(cq, wuq_ref[...])
    kv = _dot(ckv, wukv_ref[...])
    ones_row = (lax.broadcasted_iota(jnp.int32, (MLA_VT_ROWS, 1), 0) == MLA_V).astype(f32)
    for hd in range(MLA_HEADS):
        qn = q[:, hd * MLA_QK:hd * MLA_QK + MLA_NOPE]
        qr = q[:, hd * MLA_QK + MLA_NOPE:(hd + 1) * MLA_QK]
        qm_o[hd, :, :MLA_NOPE] = (qn * att_scale).astype(bf16)
        qm_o[hd, :, MLA_NOPE:] = (_rope(qr, cm, sm) * att_scale).astype(bf16)
        km_o[hd, :, :MLA_NOPE] = kv[:, hd * MLA_NOPE:(hd + 1) * MLA_NOPE].astype(bf16)
        km_o[hd, :, MLA_NOPE:] = kr
        v_lo = MLA_HEADS * MLA_NOPE + hd * MLA_VW
        vm_o[hd] = (kv[:, v_lo:v_lo + MLA_VW].T[:MLA_VT_ROWS, :] + ones_row).astype(bf16)

    _retention_rows(rq_o, rk_o, rv_o, rg_o, ret_o, state, decay, zeta, xi)


def _inproj(l, x, g, w_in, w_tail, cos, sin, qg, wuq, kvg, wukv):
    tm = TM_PROJ
    row = lambda width: pl.BlockSpec((tm, width), lambda i: (i, 0))
    headed = lambda width: pl.BlockSpec((MLA_HEADS, tm, width), lambda i: (0, i, 0))
    sds = jax.ShapeDtypeStruct
    ret_tab = pltpu.VMEM((RET_HEADS, RET_CHUNK, RET_CHUNK), f32)
    return pl.pallas_call(
        _inproj_kernel,
        grid=(SEQ // tm,),
        in_specs=[row(D_MODEL), _layer(l, (1, D_MODEL)), _layer(l, (D_MODEL, _C_KR)),
                  _layer(l, (D_MODEL, IN_TAIL_COLS)),
                  pl.BlockSpec((2, tm, LANES), lambda i: (0, i, 0)),
                  pl.BlockSpec((2, tm, LANES), lambda i: (0, i, 0)),
                  _layer(l, (1, MLA_Q_RANK)), _layer(l, (MLA_Q_RANK, MLA_HEADS * MLA_QK)),
                  _layer(l, (1, MLA_KV_RANK)), _layer(l, (MLA_KV_RANK, MLA_HEADS * (MLA_NOPE + MLA_VW)))],
        out_specs=[row(RET_WIDTH), headed(MLA_QK), headed(MLA_QK),
                   pl.BlockSpec((MLA_HEADS, MLA_VT_ROWS, tm), lambda i: (0, 0, i)), row(MEM_WIDTH)],
        out_shape=[sds((SEQ, RET_WIDTH), bf16),
                   sds((MLA_HEADS, SEQ, MLA_QK), bf16), sds((MLA_HEADS, SEQ, MLA_QK), bf16),
                   sds((MLA_HEADS, MLA_VT_ROWS, SEQ), bf16), sds((SEQ, MEM_WIDTH), bf16)],
        scratch_shapes=[pltpu.VMEM((tm, RET_WIDTH), bf16), pltpu.VMEM((tm, RET_WIDTH), bf16),
                        pltpu.VMEM((tm, RET_WIDTH), bf16), pltpu.VMEM((tm, RET_WIDTH), f32),
                        pltpu.VMEM((RET_HEADS, RET_HEAD_DIM, RET_HEAD_DIM), f32), ret_tab, ret_tab, ret_tab],
        compiler_params=_params(("arbitrary",)),
        name="in_proj",
    )(x, g, w_in, w_tail, cos, sin, qg, wuq, kvg, wukv)


def _log_gamma(hd):
    return math.log(1.0 - 2.0 ** (-5.0 - hd))


def _retention_init(state, decay, zeta, xi):
    C = RET_CHUNK
    state[...] = jnp.zeros_like(state)
    r = lax.broadcasted_iota(jnp.int32, (C, C), 0).astype(f32)
    c = lax.broadcasted_iota(jnp.int32, (C, C), 1).astype(f32)
    diff = r - c
    for hd in range(RET_HEADS):
        lg = _log_gamma(hd)
        decay[hd] = jnp.where(diff >= 0, jnp.exp(lg * jnp.maximum(diff, 0.0)), 0.0)
        zeta[hd] = jnp.exp(lg * (C - 1.0 - r))
        xi[hd] = jnp.exp(lg * (r + 1.0))


def _retention_rows(q_ref, k_ref, v_ref, g_ref, o_ref, state, decay, zeta, xi):
    C = RET_CHUNK
    n_chunks = q_ref.shape[0] // C
    heads = range(RET_HEADS)

    def block(ch, hd):
        return slice(ch * C, (ch + 1) * C), slice(hd * RET_HEAD_DIM, (hd + 1) * RET_HEAD_DIM)

    def state_free(ch):
        out = []
        for hd in heads:
            q, k, v = q_ref[block(ch, hd)], k_ref[block(ch, hd)], v_ref[block(ch, hd)]
            scores = (_nt_dot(q, k) * decay[hd]).astype(bf16)
            kz = (k.astype(f32) * zeta[hd]).astype(bf16)
            out.append((scores, _tn_dot(kz, v)))
        return out

    def finish(ch, pre):
        cross = []
        for hd in heads:
            st = state[hd]
            cross.append(_dot(q_ref[block(ch, hd)], st.astype(bf16)) * xi[hd])
            state[hd] = math.exp(_log_gamma(hd) * C) * st + pre[hd][1]
        for hd in heads:
            o = _dot(pre[hd][0], v_ref[block(ch, hd)]) + cross[hd]
            o_ref[block(ch, hd)] = (g_ref[block(ch, hd)] * _rms(o)).astype(bf16)

    pre_next = state_free(0)
    for ch in range(n_chunks):
        pre = pre_next
        if ch + 1 < n_chunks:
            pre_next = state_free(ch + 1)
        finish(ch, pre)


def _mla_kernel(q_ref, k_ref, vt_ref, o_ref, *scratch):
    T, TK, R, NCH = T_ATT, ATT_KEYS, ATT_ROWS, T_ATT // ATT_ROWS
    m_scs, acc_scs, s_scs = scratch[:NCH], scratch[NCH:2 * NCH], scratch[2 * NCH:]
    qi = pl.program_id(1)
    for c in range(NCH):
        m_scs[c][...] = jnp.full_like(m_scs[c], NEG)
        acc_scs[c][...] = jnp.zeros_like(acc_scs[c])

    def update(c, off, w):
        s_sc = s_scs[c % len(s_scs)]
        m_prev = m_scs[c][...]
        m_new = jnp.maximum(m_prev, jnp.max(s_sc[:w, :], axis=0, keepdims=True))
        p = jnp.exp2(s_sc[:w, :] - m_new).astype(bf16)
        acc_scs[c][...] = acc_scs[c][...] * jnp.exp2(m_prev - m_new) + _dot(vt_ref[:, pl.ds(off, w)], p)
        m_scs[c][...] = m_new

    def scores(c, off, w, masked):
        s = _nt_dot(k_ref[pl.ds(off, w), :], q_ref[c * R:(c + 1) * R, :])
        if masked:
            key = lax.broadcasted_iota(jnp.int32, (R, R), 0)
            qry = lax.broadcasted_iota(jnp.int32, (R, R), 1)
            if w > R:
                s_scs[c % len(s_scs)][:w - R, :] = s[:w - R, :]
            s_scs[c % len(s_scs)][w - R:w, :] = jnp.where(key <= qry, s[w - R:, :], NEG)
        else:
            s_scs[c % len(s_scs)][:w, :] = s

    def sweep(off, widths, masked):
        for c in range(min(ATT_AHEAD, NCH)):
            scores(c, off, widths[c], masked)
        for c in range(NCH):
            if c + ATT_AHEAD < NCH:
                scores(c + ATT_AHEAD, off, widths[c + ATT_AHEAD], masked)
            update(c, off, widths[c])

    def body(j, carry):
        sweep(pl.multiple_of(j * TK, TK), [TK] * NCH, False)
        return carry

    lax.fori_loop(0, qi * (T // TK), body, 0)

    sweep(pl.multiple_of(qi * T, T), [(c + 1) * R for c in range(NCH)], True)

    for c in range(NCH):
        acc = acc_scs[c][...]
        out_t = acc[:MLA_V, :] / acc[MLA_V:MLA_V + 1, :]
        o_ref[c * R:(c + 1) * R, :] = out_t.T.astype(bf16)


def _mla_attention(qm, km, vm):
    T, R = T_ATT, ATT_ROWS
    nch = T // R
    return pl.pallas_call(
        _mla_kernel,
        grid=(MLA_HEADS, SEQ // T),
        in_specs=[pl.BlockSpec((None, T, MLA_QK), lambda h, i: (h, i, 0)),
                  pl.BlockSpec((None, SEQ, MLA_QK), lambda h, i: (h, 0, 0)),
                  pl.BlockSpec((None, MLA_VT_ROWS, SEQ), lambda h, i: (h, 0, 0))],
        out_specs=pl.BlockSpec((None, T, MLA_V), lambda h, i: (h, i, 0)),
        out_shape=jax.ShapeDtypeStruct((MLA_HEADS, SEQ, MLA_V), bf16),
        scratch_shapes=([pltpu.VMEM((1, R), f32)] * nch + [pltpu.VMEM((MLA_VT_ROWS, R), f32)] * nch
                        + [pltpu.VMEM((T, R), f32)] * (ATT_AHEAD + 1)),
        compiler_params=_params(("arbitrary", "arbitrary")),
        name="mla_attention",
    )(qm, km, vm)


def _mix_kernel(x_ref, ret_ref, att_ref, mq_ref, mem_ref, memg_ref, wmem_ref, wo_ref, g_ref, o_ref, kbd_sc, vbd_sc):
    @pl.when(pl.program_id(0) == 0)
    def _():
        mn = (_rms(mem_ref[...]) * memg_ref[...]).astype(bf16)
        kv = _dot(mn, wmem_ref[...])
        k_t = kv[:, :MEM_WIDTH].T
        v = kv[:, MEM_WIDTH:]
        dim_of_row = lax.broadcasted_iota(jnp.int32, (MEM_WIDTH, N_MEM), 0) // MEM_HEAD_DIM
        dim_of_col = lax.broadcasted_iota(jnp.int32, (N_MEM, MEM_WIDTH), 1) // MEM_HEAD_DIM
        for hd in range(MEM_HEADS):
            blk = slice(hd * N_MEM, (hd + 1) * N_MEM)
            kbd_sc[:, blk] = jnp.where(dim_of_row == hd, k_t, 0.0).astype(bf16)
            vbd_sc[blk, :] = jnp.where(dim_of_col == hd, v, 0.0).astype(bf16)

    s_all = _dot(mq_ref[...], kbd_sc[...])
    att = jnp.concatenate([att_ref[hd] for hd in range(MLA_HEADS)], axis=-1)
    mixed = _dot(ret_ref[...], wo_ref[:RET_WIDTH, :])
    mixed += _dot(att, wo_ref[RET_WIDTH:RET_WIDTH + MLA_WIDTH, :])
    head_of_lane = lax.broadcasted_iota(jnp.int32, (1, MEM_WIDTH), 1) // MEM_HEAD_DIM
    probs = []
    denom = jnp.zeros((s_all.shape[0], MEM_WIDTH), f32)
    for hd in range(MEM_HEADS):
        s = s_all[:, hd * N_MEM:(hd + 1) * N_MEM]
        p = jnp.exp(s - jnp.max(s, axis=-1, keepdims=True))
        denom = jnp.where(head_of_lane == hd, jnp.sum(p, axis=-1, keepdims=True), denom)
        probs.append(p.astype(bf16))
    mo = _dot(jnp.concatenate(probs, axis=-1), vbd_sc[...]) / denom
    mixed += _dot(mo.astype(bf16), wo_ref[RET_WIDTH + MLA_WIDTH:, :])
    o_ref[...] = x_ref[...] + _rms(mixed) * g_ref[...]


def _mix(l, x, ret, att, mq, mem, memg, wmem, wo, g):
    tm = TM_PROJ
    row = lambda width: pl.BlockSpec((tm, width), lambda i: (i, 0))
    return pl.pallas_call(
        _mix_kernel,
        grid=(SEQ // tm,),
        in_specs=[row(D_MODEL), row(RET_WIDTH), pl.BlockSpec((MLA_HEADS, tm, MLA_V), lambda i: (0, i, 0)),
                  row(MEM_WIDTH), _const((N_MEM, D_MODEL)), _layer(l, (1, D_MODEL)),
                  _layer(l, (D_MODEL, 2 * MEM_WIDTH)), _layer(l, (D_MODEL, D_MODEL)), _layer(l, (1, D_MODEL))],
        out_specs=row(D_MODEL),
        out_shape=jax.ShapeDtypeStruct((SEQ, D_MODEL), f32),
        scratch_shapes=[pltpu.VMEM((MEM_WIDTH, MEM_HEADS * N_MEM), bf16),
                        pltpu.VMEM((MEM_HEADS * N_MEM, MEM_WIDTH), bf16)],
        compiler_params=_params(("arbitrary",)),
        name="mix_out_proj",
    )(x, ret, att, mq, mem, memg, wmem, wo, g)


def _ffn_kernel(x_ref, gin_ref, wg_ref, wu_ref, wd_ref, gout_ref, o_ref):
    x = x_ref[...]
    r = lax.rsqrt(jnp.mean(x * x, axis=-1, keepdims=True) + EPS)
    h = (x * gin_ref[...]).astype(bf16)
    f = jnp.zeros((x.shape[0], D_MODEL), f32)
    for c in range(D_FF // FF_CHUNK):
        cols = slice(c * FF_CHUNK, (c + 1) * FF_CHUNK)
        a = _silu(_dot(h, wg_ref[:, cols]) * r) * (_dot(h, wu_ref[:, cols]) * r)
        f += _dot(a.astype(bf16), wd_ref[cols, :])
    o_ref[...] = x + _rms(f) * gout_ref[...]


def _ffn(l, x, gin, wg, wu, wd, gout):
    tm = TM_PROJ
    row = pl.BlockSpec((tm, D_MODEL), lambda i: (i, 0))
    return pl.pallas_call(
        _ffn_kernel,
        grid=(SEQ // tm,),
        in_specs=[row, _layer(l, (1, D_MODEL)), _layer(l, (D_MODEL, D_FF)), _layer(l, (D_MODEL, D_FF)),
                  _layer(l, (D_FF, D_MODEL)), _layer(l, (1, D_MODEL))],
        out_specs=row,
        out_shape=jax.ShapeDtypeStruct((SEQ, D_MODEL), f32),
        compiler_params=_params(("parallel",)),
        name="ffn",
    )(x, gin, wg, wu, wd, gout)


def _layout_w_in(w):
    half = MLA_ROPE // 2
    wb = w.astype(bf16)
    t = wb[..., _C_KR:]
    z = jnp.zeros(t.shape[:-1] + (half,), bf16)
    tail = jnp.concatenate([t[..., :half], z, t[..., half:MLA_ROPE], z, t[..., MLA_ROPE:]], axis=-1)
    return wb, tail


def _layout_w_uq(w):
    half = MLA_ROPE // 2
    lead = w.shape[:-1]
    w = w.reshape(lead + (MLA_HEADS, MLA_NOPE + MLA_ROPE))
    z = jnp.zeros(lead + (MLA_HEADS, half), w.dtype)
    x1 = w[..., MLA_NOPE:MLA_NOPE + half]
    x2 = w[..., MLA_NOPE + half:]
    return jnp.concatenate([w[..., :MLA_NOPE], x1, z, x2, z], axis=-1).reshape(lead + (-1,)).astype(bf16)


def _layout_w_ukv(w):
    lead = w.shape[:-1]
    w = w.reshape(lead + (MLA_HEADS, MLA_NOPE + MLA_V))
    kn = w[..., :MLA_NOPE].reshape(lead + (-1,))
    v = jnp.concatenate([w[..., MLA_NOPE:], jnp.zeros(lead + (MLA_HEADS, MLA_VW - MLA_V), w.dtype)], axis=-1)
    return jnp.concatenate([kn, v.reshape(lead + (-1,))], axis=-1).astype(bf16)


def kernel(x, mem, positions, pre_mix_g, w_in, mla_q_norm_g, w_uq, mla_kv_norm_g, w_ukv, mem_norm_g, w_mem_kv,
           w_out, post_mix_g, pre_ffn_g, w_gate, w_up, w_down, post_ffn_g):
    assert x.shape == (1, SEQ, D_MODEL) and mem.shape == (1, N_MEM, D_MODEL)
    xs = x[0]
    mem2 = mem[0]
    cos, sin = _rope_tables(positions)
    vec = lambda g: g[:, None, :]
    in_args = (vec(pre_mix_g), *_layout_w_in(w_in), cos, sin, vec(mla_q_norm_g), _layout_w_uq(w_uq),
               vec(mla_kv_norm_g), _layout_w_ukv(w_ukv))
    mix_args = (mem2, vec(mem_norm_g), w_mem_kv.astype(bf16), w_out.astype(bf16), vec(post_mix_g))
    ffn_args = (vec(pre_ffn_g), w_gate.astype(bf16), w_up.astype(bf16), w_down.astype(bf16), vec(post_ffn_g))
    for l in range(DEPTH):
        ret, qm, km, vm, mq = _inproj(l, xs, *in_args)
        att = _mla_attention(qm, km, vm)
        xs = _mix(l, xs, ret, att, mq, *mix_args)
        xs = _ffn(l, xs, *ffn_args)
    return xs[None]
```
